```python
import math
import jax, jax.numpy as jnp
from jax import lax
import numpy as np

D_MODEL = 1024
BATCH = 16
SEQ = 256
DEPTH = 2
DEC_BATCH = 8
DEC_SEQ = 4096
PAST_LEN = 512

GRID_W = 64
N_EVEN = (DEPTH + 1) // 2
N_ODD = DEPTH // 2
CHUNK = 128
Q_BLOCK = 128
EPS = 1e-6
ROPE_BASE = 10000.0
A_HEADS = 4
A_QK = 64
A_V = 2 * A_QK
A_QKW = A_HEADS * 2 * A_QK
A_WIDTH = A_HEADS * A_V
N_FREQ = A_QK // 4
B_GROUPS = 4
B_CH = 128
B_WIDTH = B_GROUPS * B_CH
MIX0 = A_WIDTH + B_WIDTH
IN0 = 2 * A_QKW + A_WIDTH + 2 * B_WIDTH
C_HEADS = 4
C_INNER = D_MODEL
C_HD = C_INNER // C_HEADS
IN1 = 2 * C_INNER + 4 * C_HEADS
D_FF = ((8 * D_MODEL // 3 + 127) // 128) * 128
N_MOD = 6

kernel_name = 'hybrid_diffusion_diffattn_gmlp_mlstm_step'


def _rmsnorm(x, w):
    xf = x.astype(jnp.float32)
    y = xf * lax.rsqrt(jnp.mean(xf * xf, axis=-1, keepdims=True) + EPS)
    return (y * w.astype(jnp.float32)).astype(x.dtype)


def _ada(cond, w, b):
    mod = jax.nn.silu(cond) @ w + b
    return jnp.split(mod[..., None, :], N_MOD, axis=-1)


def _modulate(x, w, shift, scale):
    return _rmsnorm(x, w) * (1 + scale) + shift


def _dwconv3(x, w, b):
    xp = jnp.pad(x, ((0, 0), (1, 1), (0, 0)))
    return xp[:, :-2] * w[0] + xp[:, 1:-1] * w[1] + xp[:, 2:] * w[2] + b


def _axial_rope_tables(length):
    n_rows = length // GRID_W
    rows = jnp.repeat(jnp.arange(n_rows, dtype=jnp.float32), GRID_W)
    cols = jnp.tile(jnp.arange(GRID_W, dtype=jnp.float32), n_rows)
    inv = ROPE_BASE ** (-jnp.arange(N_FREQ, dtype=jnp.float32) / N_FREQ)
    ang = jnp.stack([rows[:, None] * inv, cols[:, None] * inv], axis=1)
    return jnp.cos(ang), jnp.sin(ang)


def _rope(x, cos, sin):
    xs = x.reshape(*x.shape[:-1], 2, 2, N_FREQ)
    x1, x2 = xs[..., 0, :], xs[..., 1, :]
    c = cos.reshape(cos.shape[0], 1, 1, 2, N_FREQ).astype(x.dtype)
    s = sin.reshape(sin.shape[0], 1, 1, 2, N_FREQ).astype(x.dtype)
    out = jnp.stack([x1 * c - x2 * s, x2 * c + x1 * s], axis=-2)
    return out.reshape(x.shape)


def _diff_attention(q, k, v, lam):
    Bn, Lq = q.shape[:2]
    nb = Lq // Q_BLOCK
    qb = jnp.moveaxis(q.reshape(Bn, nb, Q_BLOCK, *q.shape[2:]), 1, 0)
    scale = A_QK ** -0.5

    def block(qi):
        s = jnp.einsum('bqhmd,bkhmd->mbhqk', qi, k).astype(jnp.float32) * scale
        p = jax.nn.softmax(s, axis=-1)
        w = (p[0] - lam * p[1]).astype(v.dtype)
        return jnp.einsum('bhqk,bkhe->bqhe', w, v)

    out = lax.map(block, qb)
    return jnp.moveaxis(out, 0, 1).reshape(Bn, Lq, A_HEADS, A_V)


def _chunk_gmlp(u, v, norm_w, w_s, b_s):
    Bn, L, _ = u.shape
    vc = _rmsnorm(v, norm_w).reshape(Bn, L // CHUNK, CHUNK, B_GROUPS, B_CH)
    s = jnp.einsum('gts,bnsgc->bntgc', w_s, vc) + b_s.T[:, :, None]
    return u * s.reshape(Bn, L, B_WIDTH)


def _even_mixer(h, layer_idx, w_in, lq1, lk1, lq2, lk2, subln_w, gn_w, w_s, b_s, w_out,
                rope=None, ctx_k=None, ctx_v=None):
    Bn, L, _ = h.shape
    z = h @ w_in
    q, k, v, gu, gv = jnp.split(z, [A_QKW, 2 * A_QKW, 2 * A_QKW + A_WIDTH,
                                    2 * A_QKW + A_WIDTH + B_WIDTH], axis=-1)
    q = q.reshape(Bn, L, A_HEADS, 2, A_QK)
    k = k.reshape(Bn, L, A_HEADS, 2, A_QK)
    v = v.reshape(Bn, L, A_HEADS, A_V)
    k_own, v_own = k, v
    if rope is not None:
        q = _rope(q, *rope)
        k = jnp.concatenate([ctx_k.reshape(Bn, -1, A_HEADS, 2, A_QK).astype(k.dtype),
                             _rope(k, *rope)], axis=1)
        v = jnp.concatenate([ctx_v.astype(v.dtype), v], axis=1)
    lam_init = 0.8 - 0.6 * math.exp(-0.3 * layer_idx)
    f32 = jnp.float32
    lam = (jnp.exp(jnp.sum(lq1.astype(f32) * lk1.astype(f32)))
           - jnp.exp(jnp.sum(lq2.astype(f32) * lk2.astype(f32))) + lam_init)
    a = _diff_attention(q, k, v, lam)
    a = (_rmsnorm(a, subln_w) * (1 - lam_init)).reshape(Bn, L, A_WIDTH)
    g = _chunk_gmlp(jax.nn.gelu(gu), jax.nn.gelu(gv), gn_w, w_s, b_s)
    y = jnp.concatenate([a, g], axis=-1) @ w_out
    return y, k_own.reshape(Bn, L, A_HEADS, 2 * A_QK), v_own


def _mlstm_scan(q, k, v, li, lf, C0, n0, m0):
    Bn, H, L, _ = q.shape
    nc = L // CHUNK

    def chunks(t):
        return jnp.moveaxis(t.reshape(Bn, H, nc, CHUNK, *t.shape[3:]), 2, 0)

    lower = jnp.tril(jnp.ones((CHUNK, CHUNK), dtype=bool))

    def step(carry, xs):
        C, n, m = carry
        qc, kc, vc, ic, fc = xs
        b = jnp.cumsum(fc, axis=-1)
        a_inter = b + m[..., None]
        d = jnp.where(lower, b[..., :, None] - b[..., None, :] + ic[..., None, :], -jnp.inf)
        m_t = jnp.maximum(a_inter, jnp.max(d, axis=-1))
        w_inter = jnp.exp(a_inter - m_t)
        s = jnp.einsum('bhtd,bhsd->bhts', qc, kc) * jnp.exp(d - m_t[..., None])
        num = (jnp.einsum('bhts,bhse->bhte', s, vc)
               + w_inter[..., None] * jnp.einsum('bhtd,bhde->bhte', qc, C))
        den = jnp.sum(s, axis=-1) + w_inter * jnp.einsum('bhtd,bhd->bht', qc, n)
        h = num / jnp.maximum(jnp.abs(den), jnp.exp(-m_t))[..., None]
        b_end = b[..., -1]
        g = b_end[..., None] - b + ic
        m_new = jnp.maximum(b_end + m, jnp.max(g, axis=-1))
        decay = jnp.exp(b_end + m - m_new)
        ws = jnp.exp(g - m_new[..., None])
        C_new = decay[..., None, None] * C + jnp.einsum('bhs,bhsd,bhse->bhde', ws, kc, vc)
        n_new = decay[..., None] * n + jnp.einsum('bhs,bhsd->bhd', ws, kc)
        return (C_new, n_new, m_new), h

    (C, n, m), hs = lax.scan(step, (C0, n0, m0),
                             (chunks(q), chunks(k), chunks(v), chunks(li), chunks(lf)))
    return jnp.moveaxis(hs, 0, 2).reshape(Bn, H, L, -1), C, n, m


def _mlstm_bidir(q, k, v, gates, init_f, init_b):
    rev = lambda t: jnp.flip(t, axis=2)
    h_f, C_f, n_f, m_f = _mlstm_scan(q, k, v, gates[0], jax.nn.log_sigmoid(gates[1]), *init_f)
    h_b, C_b, n_b, m_b = _mlstm_scan(rev(q), rev(k), rev(v), rev(gates[2]),
                                     rev(jax.nn.log_sigmoid(gates[3])), *init_b)
    return h_f + rev(h_b), (C_f, n_f, m_f), (C_b, n_b, m_b)


def _odd_mixer(h, init_f, init_b, w_in, b_g, cw, cb, wq, wk, wv, hn_w, skip, w_out):
    Bn, L, _ = h.shape
    f32 = jnp.float32
    z = h @ w_in
    xm, og, g = jnp.split(z, [C_INNER, 2 * C_INNER], axis=-1)
    xc = jax.nn.silu(_dwconv3(xm, cw, cb))
    heads = lambda t: t.reshape(Bn, L, C_HEADS, C_HD)
    q = jnp.einsum('blhd,hde->bhle', heads(xc), wq).astype(f32)
    k = (jnp.einsum('blhd,hde->bhle', heads(xc), wk) * C_HD ** -0.5).astype(f32)
    v = jnp.einsum('blhd,hde->bhle', heads(xm), wv).astype(f32)
    gates = (g + b_g).astype(f32).reshape(Bn, L, 4, C_HEADS).transpose(2, 0, 3, 1)
    hsum, st_f, st_b = _mlstm_bidir(q, k, v, gates, init_f, init_b)
    hsum = hsum.transpose(0, 2, 1, 3).astype(h.dtype)
    hn = _rmsnorm(hsum, hn_w.reshape(C_HEADS, C_HD)).reshape(Bn, L, C_INNER)
    y = jax.nn.sigmoid(og) * (hn + skip * xc)
    return y @ w_out, st_f, st_b


def _conv_ffn(h, w_up, cw, cb, w_down):
    a, g = jnp.split(h @ w_up, 2, axis=-1)
    return (jax.nn.gelu(_dwconv3(a, cw, cb)) * g) @ w_down


def setup_inputs(seed: int = 0) -> dict:
    key = jax.random.key(seed)
    ks = jax.random.split(key, 48)
    nrm = lambda k, shape, scale: scale * jax.random.normal(k, shape, jnp.float32)
    gain = lambda k, shape: 1.0 + 0.05 * jax.random.normal(k, shape, jnp.float32)
    d = D_MODEL
    f_bias = jnp.broadcast_to(jnp.linspace(3.0, 6.0, C_HEADS, dtype=jnp.float32), (N_ODD, C_HEADS))
    b_gates = jnp.concatenate([
        nrm(ks[20], (N_ODD, C_HEADS), 0.1),
        f_bias + nrm(ks[21], (N_ODD, C_HEADS), 0.01),
        nrm(ks[22], (N_ODD, C_HEADS), 0.1),
        f_bias + nrm(ks[23], (N_ODD, C_HEADS), 0.01)], axis=-1)
    return {
        'x_prompt': nrm(ks[0], (BATCH, SEQ, d), 1.0),
        'x_sample': nrm(ks[1], (DEC_BATCH, DEC_SEQ, d), 1.0),
        'cache_k': nrm(ks[2], (DEC_BATCH, N_EVEN, PAST_LEN, A_HEADS, 2 * A_QK), 1.0),
        'cache_v': nrm(ks[3], (DEC_BATCH, N_EVEN, PAST_LEN, A_HEADS, A_V), 1.0),
        'state_C': nrm(ks[4], (DEC_BATCH, N_ODD, 2, C_HEADS, C_HD, C_HD), 0.05),
        'state_n': nrm(ks[5], (DEC_BATCH, N_ODD, 2, C_HEADS, C_HD), 0.05),
        'state_m': nrm(ks[6], (DEC_BATCH, N_ODD, 2, C_HEADS), 0.5),
        'c': nrm(ks[7], (DEC_BATCH, d), 1.0),
        'c_ctx': nrm(ks[8], (d,), 1.0),
        'w_mod': nrm(ks[9], (DEPTH, d, N_MOD * d), 0.5 * d ** -0.5),
        'b_mod': nrm(ks[10], (DEPTH, N_MOD * d), 0.02),
        'norm1_w': gain(ks[11], (DEPTH, d)),
        'norm2_w': gain(ks[12], (DEPTH, d)),
        'w_in0': nrm(ks[13], (N_EVEN, d, IN0), d ** -0.5),
        'lam_q1': nrm(ks[14], (N_EVEN, A_QK), 0.1),
        'lam_k1': nrm(ks[15], (N_EVEN, A_QK), 0.1),
        'lam_q2': nrm(ks[16], (N_EVEN, A_QK), 0.1),
        'lam_k2': nrm(ks[17], (N_EVEN, A_QK), 0.1),
        'subln_w': gain(ks[18], (N_EVEN, A_V)),
        'gate_norm_w': gain(ks[19], (N_EVEN, B_WIDTH)),
        'w_spatial': nrm(ks[24], (N_EVEN, B_GROUPS, CHUNK, CHUNK), CHUNK ** -0.5),
        'b_spatial': 1.0 + nrm(ks[25], (N_EVEN, B_GROUPS, CHUNK), 0.02),
        'w_out0': nrm(ks[26], (N_EVEN, MIX0, d), MIX0 ** -0.5),
        'w_in1': nrm(ks[27], (N_ODD, d, IN1), d ** -0.5),
        'b_gates': b_gates,
        'mconv_w': nrm(ks[28], (N_ODD, 3, C_INNER), 0.5),
        'mconv_b': nrm(ks[29], (N_ODD, C_INNER), 0.02),
        'w_q': nrm(ks[30], (N_ODD, C_HEADS, C_HD, C_HD), C_HD ** -0.5),
        'w_k': nrm(ks[31], (N_ODD, C_HEADS, C_HD, C_HD), C_HD ** -0.5),
        'w_v': nrm(ks[32], (N_ODD, C_HEADS, C_HD, C_HD), C_HD ** -0.5),
        'head_norm_w': gain(ks[33], (N_ODD, C_INNER)),
        'skip_w': gain(ks[34], (N_ODD, C_INNER)),
        'w_out1': nrm(ks[35], (N_ODD, C_INNER, d), C_INNER ** -0.5),
        'w_up': nrm(ks[36], (DEPTH, d, 2 * D_FF), d ** -0.5),
        'fconv_w': nrm(ks[37], (DEPTH, 3, D_FF), 0.5),
        'fconv_b': nrm(ks[38], (DEPTH, D_FF), 0.02),
        'w_down': nrm(ks[39], (DEPTH, D_FF, d), D_FF ** -0.5),
        'final_norm_w': gain(ks[40], (d,)),
    }


def reference(x_prompt, x_sample, cache_k, cache_v, state_C, state_n, state_m, c, c_ctx,
              w_mod, b_mod, norm1_w, norm2_w,
              w_in0, lam_q1, lam_k1, lam_q2, lam_k2, subln_w, gate_norm_w, w_spatial, b_spatial, w_out0,
              w_in1, b_gates, mconv_w, mconv_b, w_q, w_k, w_v, head_norm_w, skip_w, w_out1,
              w_up, fconv_w, fconv_b, w_down, final_norm_w):
    f32 = jnp.float32

    x = x_prompt
    Bp = x_prompt.shape[0]
    ks, vs, Cs, ns, ms = [], [], [], [], []
    for l in range(DEPTH):
        sh1, sc1, g1, sh2, sc2, g2 = _ada(c_ctx, w_mod[l], b_mod[l])
        h = _modulate(x, norm1_w[l], sh1, sc1)
        if l % 2 == 0:
            e = l // 2
            y, k_ctx, v_ctx = _even_mixer(h, l, w_in0[e], lam_q1[e], lam_k1[e], lam_q2[e], lam_k2[e],
                                          subln_w[e], gate_norm_w[e], w_spatial[e], b_spatial[e], w_out0[e])
            ks.append(k_ctx)
            vs.append(v_ctx)
        else:
            o = l // 2
            zero = (jnp.zeros((Bp, C_HEADS, C_HD, C_HD), f32), jnp.zeros((Bp, C_HEADS, C_HD), f32),
                    jnp.zeros((Bp, C_HEADS), f32))
            y, st_f, st_b = _odd_mixer(h, zero, zero, w_in1[o], b_gates[o], mconv_w[o], mconv_b[o],
                                       w_q[o], w_k[o], w_v[o], head_norm_w[o], skip_w[o], w_out1[o])
            Cs.append(jnp.stack([st_f[0], st_b[0]], axis=1))
            ns.append(jnp.stack([st_f[1], st_b[1]], axis=1))
            ms.append(jnp.stack([st_f[2], st_b[2]], axis=1))
        x = x + g1 * y
        h = _modulate(x, norm2_w[l], sh2, sc2)
        x = x + g2 * _conv_ffn(h, w_up[l], fconv_w[l], fconv_b[l], w_down[l])
    y_prompt = _rmsnorm(x, final_norm_w)
    new_cache_k = jnp.stack(ks, axis=1)
    new_cache_v = jnp.stack(vs, axis=1)
    new_state_C = jnp.stack(Cs, axis=1)
    new_state_n = jnp.stack(ns, axis=1)
    new_state_m = jnp.stack(ms, axis=1)

    x = x_sample
    rope = _axial_rope_tables(x_sample.shape[1])
    for l in range(DEPTH):
        sh1, sc1, g1, sh2, sc2, g2 = _ada(c, w_mod[l], b_mod[l])
        h = _modulate(x, norm1_w[l], sh1, sc1)
        if l % 2 == 0:
            e = l // 2
            y, _, _ = _even_mixer(h, l, w_in0[e], lam_q1[e], lam_k1[e], lam_q2[e], lam_k2[e],
                                  subln_w[e], gate_norm_w[e], w_spatial[e], b_spatial[e], w_out0[e],
                                  rope=rope, ctx_k=cache_k[:, e], ctx_v=cache_v[:, e])
        else:
            o = l // 2
            init_f = (state_C[:, o, 0].astype(f32), state_n[:, o, 0].astype(f32), state_m[:, o, 0].astype(f32))
            init_b = (state_C[:, o, 1].astype(f32), state_n[:, o, 1].astype(f32), state_m[:, o, 1].astype(f32))
            y, _, _ = _odd_mixer(h, init_f, init_b, w_in1[o], b_gates[o], mconv_w[o], mconv_b[o],
                                 w_q[o], w_k[o], w_v[o], head_norm_w[o], skip_w[o], w_out1[o])
        x = x + g1 * y
        h = _modulate(x, norm2_w[l], sh2, sc2)
        x = x + g2 * _conv_ffn(h, w_up[l], fconv_w[l], fconv_b[l], w_down[l])
    y_sample = _rmsnorm(x, final_norm_w)

    return (y_prompt, y_sample, new_cache_k, new_cache_v, new_state_C, new_state_n, new_state_m)
```

```python
import functools
import math

import jax
import jax.numpy as jnp
from jax import lax
from jax.experimental import pallas as pl
from jax.experimental.pallas import tpu as pltpu

F32 = jnp.float32
BF16 = jnp.bfloat16

EPS = 1e-6
ROPE_BASE = 10000.0
GRID_W = 64
N_MOD = 6
A_HEADS = 4
A_QK = 64
A_V = 2 * A_QK
A_WIDTH = A_HEADS * A_V
N_FREQ = A_QK // 4
B_GROUPS = 4
CHUNK = 128
C_HEADS = 4
LOG2E = 1.4426950408889634

HALO = 8
VMEM_LIMIT = 56 * 1024 * 1024
NEG_BIG = -1e30


def _cparams(*sem):
    return pltpu.CompilerParams(dimension_semantics=sem, vmem_limit_bytes=VMEM_LIMIT)


def _gelu(x):
    return 0.5 * x * (1.0 + jnp.tanh(math.sqrt(2.0 / math.pi) * (x + 0.044715 * (x * x * x))))


def _sigmoid(x):
    return 1.0 / (1.0 + jnp.exp(-x))


def _log_sigmoid(x):
    return jnp.minimum(x, 0.0) - jnp.log1p(jnp.exp(-jnp.abs(x)))


def _rms(x, w):
    return x * lax.rsqrt(jnp.mean(x * x, axis=-1, keepdims=True) + EPS) * w


def _modulated(x, nw, shift, scale):
    return _rms(x, nw) * (1.0 + scale) + shift


def _dot(a, b):
    return jnp.dot(a, b, preferred_element_type=F32)


def _conv3(xe, tm, seq_len, row0, w_ref, b_ref):
    n = xe.shape[0]
    prev = pltpu.roll(xe, 1, 0)[HALO:HALO + tm]
    nxt = pltpu.roll(xe, n - 1, 0)[HALO:HALO + tm]
    cur = xe[HALO:HALO + tm]
    pos = (row0 + lax.broadcasted_iota(jnp.int32, (tm, 1), 0)) % seq_len
    prev = jnp.where(pos == 0, 0.0, prev)
    nxt = jnp.where(pos == seq_len - 1, 0.0, nxt)
    return prev * w_ref[0:1, :] + cur * w_ref[1:2, :] + nxt * w_ref[2:3, :] + b_ref[...]


def _mod_body(c_ref, w_ref, b_ref, o_ref):
    c = c_ref[...]
    s = (c * _sigmoid(c)).astype(BF16)
    o_ref[0] = _dot(s, w_ref[0].astype(BF16)) + b_ref[0]


def _modulation(cond, w_mod, b_mod):
    depth, d, n = w_mod.shape
    r = cond.shape[0]
    tn = n // 4
    return pl.pallas_call(
        _mod_body,
        grid=(depth, n // tn),
        in_specs=[pl.BlockSpec((r, d), lambda l, j: (0, 0)),
                  pl.BlockSpec((1, d, tn), lambda l, j: (l, 0, j)),
                  pl.BlockSpec((1, 1, tn), lambda l, j: (l, 0, j))],
        out_specs=pl.BlockSpec((1, r, tn), lambda l, j: (l, 0, j)),
        out_shape=jax.ShapeDtypeStruct((depth, r, n), F32),
        compiler_params=_cparams("parallel", "parallel"),
        name="modulation",
    )(cond, w_mod, b_mod.reshape(depth, 1, n))


def _in_proj_body(x_ref, nw_ref, mod_ref, w_ref, o_ref):
    h = _modulated(x_ref[...], nw_ref[...], mod_ref[0, 0:1, :], mod_ref[0, 1:2, :])
    o_ref[...] = _dot(h.astype(BF16), w_ref[...]).astype(o_ref.dtype)


def _in_proj(x, nw, mod, w, rows_per_mod, tm):
    m, d = x.shape
    n = w.shape[1]
    return pl.pallas_call(
        _in_proj_body,
        grid=(m // tm,),
        in_specs=[pl.BlockSpec((tm, d), lambda i: (i, 0)),
                  pl.BlockSpec((1, d), lambda i: (0, 0)),
                  pl.BlockSpec((1, N_MOD, d), lambda i: ((i * tm) // rows_per_mod, 0, 0)),
                  pl.BlockSpec((d, n), lambda i: (0, 0))],
        out_specs=pl.BlockSpec((tm, n), lambda i: (i, 0)),
        out_shape=jax.ShapeDtypeStruct((m, n), F32),
        compiler_params=_cparams("parallel"),
        name="in_proj",
    )(x, nw.reshape(1, d), mod, w)


def _qkv_prep_body(*refs, rope):
    if rope:
        q_ref, k_ref, v_ref, cos_ref, sa_ref, sb_ref, qo_ref, kt_ref, vo_ref = refs
    else:
        q_ref, k_ref, v_ref, qo_ref, kt_ref, vo_ref = refs
    q = q_ref[0]
    k = k_ref[0]
    if rope:
        w = q.shape[-1]

        def rot(x):
            return (x * cos_ref[...] + pltpu.roll(x, w - N_FREQ, 1) * sa_ref[...]
                    + pltpu.roll(x, N_FREQ, 1) * sb_ref[...])

        q = rot(q)
        k = rot(k)
    qo_ref[0] = (q * (A_QK ** -0.5 * LOG2E)).astype(BF16)
    kt_ref[0] = k.T.astype(BF16)
    vo_ref[0] = v_ref[0].astype(BF16)


def _qkv_prep(z, rope_tabs, tl):
    b, l, _ = z.shape
    w = A_WIDTH
    rope = rope_tabs is not None
    in_specs = [pl.BlockSpec((1, tl, w), lambda bi, j: (bi, j, 0)),
                pl.BlockSpec((1, tl, w), lambda bi, j: (bi, j, 1)),
                pl.BlockSpec((1, tl, w), lambda bi, j: (bi, j, 2))]
    args = [z, z, z]
    if rope:
        in_specs += [pl.BlockSpec((tl, w), lambda bi, j: (j, 0))] * 3
        args += list(rope_tabs)
    return pl.pallas_call(
        functools.partial(_qkv_prep_body, rope=rope),
        grid=(b, l // tl),
        in_specs=in_specs,
        out_specs=[pl.BlockSpec((1, tl, w), lambda bi, j: (bi, j, 0)),
                   pl.BlockSpec((1, w, tl), lambda bi, j: (bi, 0, j)),
                   pl.BlockSpec((1, tl, w), lambda bi, j: (bi, j, 0))],
        out_shape=[jax.ShapeDtypeStruct((b, l, w), BF16),
                   jax.ShapeDtypeStruct((b, w, l), BF16),
                   jax.ShapeDtypeStruct((b, l, w), BF16)],
        compiler_params=_cparams("parallel", "parallel"),
        name="qkv_prep",
    )(*args)


def _ctx_prep_body(k_ref, v_ref, kt_ref, vo_ref):
    kt_ref[0] = k_ref[0].T.astype(BF16)
    vo_ref[0] = v_ref[0].astype(BF16)


def _ctx_prep(ctx_k, ctx_v):
    b, p, w = ctx_k.shape
    return pl.pallas_call(
        _ctx_prep_body,
        grid=(b,),
        in_specs=[pl.BlockSpec((1, p, w), lambda bi: (bi, 0, 0))] * 2,
        out_specs=[pl.BlockSpec((1, w, p), lambda bi: (bi, 0, 0)),
                   pl.BlockSpec((1, p, w), lambda bi: (bi, 0, 0))],
        out_shape=[jax.ShapeDtypeStruct((b, w, p), BF16), jax.ShapeDtypeStruct((b, p, w), BF16)],
        compiler_params=_cparams("parallel"),
        name="ctx_prep",
    )(ctx_k, ctx_v)


def _attn_body(*refs, has_ctx, tk, lam_init):
    if has_ctx:
        q_ref, kt_ref, v_ref, ckt_ref, cv_ref, lam_ref, sw_ref, o_ref = refs
    else:
        q_ref, kt_ref, v_ref, lam_ref, sw_ref, o_ref = refs
    q = q_ref[0]
    tq = q.shape[0]
    lane = lax.broadcasted_iota(jnp.int32, q.shape, 1)
    zero = jnp.zeros_like(q)
    qq = jnp.concatenate([jnp.where(lane < A_QK, q, zero), jnp.where(lane >= A_QK, q, zero)], axis=0)

    m = jnp.full((2 * tq, 1), NEG_BIG, F32)
    l = jnp.zeros((2 * tq, 1), F32)
    acc = jnp.zeros((2 * tq, A_V), F32)

    sources = ([(ckt_ref, cv_ref)] if has_ctx else []) + [(kt_ref, v_ref)]
    for k_src, v_src in sources:
        lk = k_src.shape[-1]
        for c in range(lk // tk):
            kt = k_src[0, :, c * tk:(c + 1) * tk]
            vv = v_src[0, c * tk:(c + 1) * tk, :]
            s = _dot(qq, kt)
            m_new = jnp.maximum(m, jnp.max(s, axis=-1, keepdims=True))
            alpha = jnp.exp2(m - m_new)
            p = jnp.exp2(s - m_new)
            l = alpha * l + jnp.sum(p, axis=-1, keepdims=True)
            acc = alpha * acc + _dot(p.astype(BF16), vv)
            m = m_new

    o = acc / l
    lam = (jnp.exp(jnp.sum(lam_ref[0:1, :] * lam_ref[1:2, :], axis=-1, keepdims=True))
           - jnp.exp(jnp.sum(lam_ref[2:3, :] * lam_ref[3:4, :], axis=-1, keepdims=True)) + lam_init)
    a = o[:tq] - lam * o[tq:]
    o_ref[0] = (_rms(a, sw_ref[...]) * (1.0 - lam_init)).astype(o_ref.dtype)


def _attention(q, kt, v, ctx, lam4, subln_w, lam_init, tq, tk):
    b, l, w = q.shape
    lk = kt.shape[-1]
    has_ctx = ctx is not None
    in_specs = [pl.BlockSpec((1, tq, A_V), lambda bi, h, i: (bi, i, h)),
                pl.BlockSpec((1, A_V, lk), lambda bi, h, i: (bi, h, 0)),
                pl.BlockSpec((1, lk, A_V), lambda bi, h, i: (bi, 0, h))]
    args = [q, kt, v]
    if has_ctx:
        p = ctx[0].shape[-1]
        in_specs += [pl.BlockSpec((1, A_V, p), lambda bi, h, i: (bi, h, 0)),
                     pl.BlockSpec((1, p, A_V), lambda bi, h, i: (bi, 0, h))]
        args += list(ctx)
    in_specs += [pl.BlockSpec((4, A_QK), lambda bi, h, i: (0, 0)),
                 pl.BlockSpec((1, A_V), lambda bi, h, i: (0, 0))]
    args += [lam4, subln_w.reshape(1, A_V)]
    return pl.pallas_call(
        functools.partial(_attn_body, has_ctx=has_ctx, tk=tk, lam_init=lam_init),
        grid=(b, A_HEADS, l // tq),
        in_specs=in_specs,
        out_specs=pl.BlockSpec((1, tq, A_V), lambda bi, h, i: (bi, i, h)),
        out_shape=jax.ShapeDtypeStruct((b, l, w), BF16),
        compiler_params=_cparams("parallel", "parallel", "parallel"),
        name="diff_attention",
    )(*args)


def _out_proj0_body(a_ref, gu_ref, gv_ref, gnw_ref, ws_ref, bs_ref, wo_ref, x_ref, mod_ref, o_ref):
    tm = x_ref.shape[0]
    bw = gu_ref.shape[1]
    ch = bw // B_GROUPS
    nch = tm // CHUNK
    u = _gelu(gu_ref[...])
    vn = _rms(_gelu(gv_ref[...]), gnw_ref[...]).astype(BF16)
    parts = []
    for g in range(B_GROUPS):
        rhs = jnp.concatenate([vn[r * CHUNK:(r + 1) * CHUNK, g * ch:(g + 1) * ch] for r in range(nch)], axis=1)
        parts.append(_dot(ws_ref[g], rhs) + bs_ref[:, g:g + 1])
    s = jnp.concatenate(
        [jnp.concatenate([parts[g][:, r * ch:(r + 1) * ch] for g in range(B_GROUPS)], axis=1) for r in range(nch)],
        axis=0)
    gg = (u * s).astype(BF16)
    aw = a_ref.shape[1]
    y = _dot(a_ref[...], wo_ref[0:aw, :]) + _dot(gg, wo_ref[aw:, :])
    o_ref[...] = x_ref[...] + mod_ref[0, 2:3, :] * y


def _out_proj0(a, z, gnw, ws, bs_t, wo, x, mod, rows_per_mod, tm):
    m, d = x.shape
    aw = a.shape[1]
    bw = gnw.shape[0]
    ub = aw * 3 // bw
    return pl.pallas_call(
        _out_proj0_body,
        grid=(m // tm,),
        in_specs=[pl.BlockSpec((tm, aw), lambda i: (i, 0)),
                  pl.BlockSpec((tm, bw), lambda i: (i, ub)),
                  pl.BlockSpec((tm, bw), lambda i: (i, ub + 1)),
                  pl.BlockSpec((1, bw), lambda i: (0, 0)),
                  pl.BlockSpec(ws.shape, lambda i: (0, 0, 0)),
                  pl.BlockSpec(bs_t.shape, lambda i: (0, 0)),
                  pl.BlockSpec(wo.shape, lambda i: (0, 0)),
                  pl.BlockSpec((tm, d), lambda i: (i, 0)),
                  pl.BlockSpec((1, N_MOD, d), lambda i: ((i * tm) // rows_per_mod, 0, 0))],
        out_specs=pl.BlockSpec((tm, d), lambda i: (i, 0)),
        out_shape=jax.ShapeDtypeStruct((m, d), F32),
        compiler_params=_cparams("parallel"),
        name="gmlp_out_proj",
    )(a, z, z, gnw.reshape(1, bw), ws, bs_t, wo, x, mod)


def _ffn_body(*refs, seq_len, fc, final):
    if final:
        x_ref, xp_ref, xn_ref, nw_ref, mod_ref, wa_ref, wg_ref, cw_ref, cb_ref, wd_ref, fw_ref, o_ref = refs
    else:
        x_ref, xp_ref, xn_ref, nw_ref, mod_ref, wa_ref, wg_ref, cw_ref, cb_ref, wd_ref, o_ref = refs
    tm, d = x_ref.shape
    dff = wa_ref.shape[1]
    x = x_ref[...]
    xe = jnp.concatenate([xp_ref[...], x, xn_ref[...]], axis=0)
    he = _modulated(xe, nw_ref[...], mod_ref[0, 3:4, :], mod_ref[0, 4:5, :]).astype(BF16)
    hc = he[HALO:HALO + tm]
    row0 = pl.program_id(0) * tm
    acc = jnp.zeros((tm, d), F32)
    for c in range(dff // fc):
        cs = slice(c * fc, (c + 1) * fc)
        ae = _dot(he, wa_ref[:, cs])
        g = _dot(hc, wg_ref[:, cs])
        conv = _conv3(ae, tm, seq_len, row0, cw_ref.at[:, cs], cb_ref.at[:, cs])
        acc = acc + _dot((_gelu(conv) * g).astype(BF16), wd_ref[cs, :])
    out = x + mod_ref[0, 5:6, :] * acc
    if final:
        out = _rms(out, fw_ref[...])
    o_ref[...] = out


def _halo_specs(tm, m, width, col):
    nb = tm // HALO
    last = m // HALO - 1
    return [pl.BlockSpec((HALO, width), lambda i: (jnp.maximum(i * nb - 1, 0), col)),
            pl.BlockSpec((HALO, width), lambda i: (jnp.minimum((i + 1) * nb, last), col))]


def _conv_ffn(x, nw, mod, wa, wg, cw, cb, wd, fw, rows_per_mod, seq_len, tm, fc):
    m, d = x.shape
    dff = wa.shape[1]
    final = fw is not None
    const = lambda i: (0, 0)
    in_specs = ([pl.BlockSpec((tm, d), lambda i: (i, 0))] + _halo_specs(tm, m, d, 0)
                + [pl.BlockSpec((1, d), const),
                   pl.BlockSpec((1, N_MOD, d), lambda i: ((i * tm) // rows_per_mod, 0, 0)),
                   pl.BlockSpec((d, dff), const),
                   pl.BlockSpec((d, dff), const),
                   pl.BlockSpec((3, dff), const),
                   pl.BlockSpec((1, dff), const),
                   pl.BlockSpec((dff, d), const)])
    args = [x, x, x, nw.reshape(1, d), mod, wa, wg, cw, cb.reshape(1, dff), wd]
    if final:
        in_specs.append(pl.BlockSpec((1, d), const))
        args.append(fw.reshape(1, d))
    return pl.pallas_call(
        functools.partial(_ffn_body, seq_len=seq_len, fc=fc, final=final),
        grid=(m // tm,),
        in_specs=in_specs,
        out_specs=pl.BlockSpec((tm, d), lambda i: (i, 0)),
        out_shape=jax.ShapeDtypeStruct((m, d), F32),
        compiler_params=_cparams("parallel"),
        name="conv_ffn",
    )(*args)


def _mlstm_qkv_body(xm_ref, xp_ref, xn_ref, cw_ref, cb_ref, wq_ref, wk_ref, wv_ref,
                    q_ref, k_ref, v_ref, xc_ref, *, seq_len):
    tm, ci = xm_ref.shape
    hd = ci // C_HEADS
    xm = xm_ref[...]
    xe = jnp.concatenate([xp_ref[...], xm, xn_ref[...]], axis=0)
    conv = _conv3(xe, tm, seq_len, pl.program_id(0) * tm, cw_ref, cb_ref)
    xc = conv * _sigmoid(conv)
    xc_ref[...] = xc
    xcb = xc.astype(BF16)
    xmb = xm.astype(BF16)
    for h in range(C_HEADS):
        hs = slice(h * hd, (h + 1) * hd)
        q_ref[:, hs] = _dot(xcb[:, hs], wq_ref[h]).astype(BF16)
        k_ref[:, hs] = (_dot(xcb[:, hs], wk_ref[h]) * hd ** -0.5).astype(BF16)
        v_ref[:, hs] = _dot(xmb[:, hs], wv_ref[h]).astype(BF16)


def _mlstm_qkv(z, ci, cw, cb, wq, wk, wv, seq_len, tm):
    m = z.shape[0]
    const3 = lambda i: (0, 0, 0)
    row = pl.BlockSpec((tm, ci), lambda i: (i, 0))
    return pl.pallas_call(
        functools.partial(_mlstm_qkv_body, seq_len=seq_len),
        grid=(m // tm,),
        in_specs=[row] + _halo_specs(tm, m, ci, 0)
        + [pl.BlockSpec((3, ci), lambda i: (0, 0)),
           pl.BlockSpec((1, ci), lambda i: (0, 0)),
           pl.BlockSpec(wq.shape, const3), pl.BlockSpec(wk.shape, const3), pl.BlockSpec(wv.shape, const3)],
        out_specs=[row, row, row, row],
        out_shape=[jax.ShapeDtypeStruct((m, ci), BF16)] * 3 + [jax.ShapeDtypeStruct((m, ci), F32)],
        compiler_params=_cparams("parallel"),
        name="mlstm_qkv",
    )(z, z, z, cw, cb.reshape(1, ci), wq, wk, wv)


def _split3(x):
    hi = x.astype(BF16)
    r1 = x - hi.astype(F32)
    mid = r1.astype(BF16)
    lo = (r1 - mid.astype(F32)).astype(BF16)
    return hi, mid, lo


def _pick_col(x, j):
    lane = lax.broadcasted_iota(jnp.int32, x.shape, 1)
    return jnp.sum(jnp.where(lane == j, x, 0.0), axis=1, keepdims=True)


def _pick_row(x, j):
    sub = lax.broadcasted_iota(jnp.int32, x.shape, 0)
    return jnp.sum(jnp.where(sub == j, x, 0.0), axis=0, keepdims=True)


def _mlstm_scan_body(*refs, zero_init):
    if zero_init:
        q_ref, k_ref, v_ref, g_ref, bg_ref, h_ref, co_ref, no_ref, mo_ref, c_s, n_s = refs
    else:
        (q_ref, k_ref, v_ref, g_ref, bg_ref, c0_ref, n0_ref, m0_ref,
         h_ref, co_ref, no_ref, mo_ref, c_s, n_s) = refs
    seq = q_ref.shape[1]
    t = CHUNK
    nc = seq // t
    head = pl.program_id(1)
    dr = pl.program_id(2)
    col_i = 2 * dr * C_HEADS + head
    col_f = (2 * dr + 1) * C_HEADS + head

    rr = lax.broadcasted_iota(jnp.int32, (t, t), 0)
    cc = lax.broadcasted_iota(jnp.int32, (t, t), 1)
    mask = (rr - cc) * (1 - 2 * dr) >= 0
    tri = jnp.where(mask, 1.0, 0.0).astype(BF16)
    end_lane = (1 - dr) * (t - 1)

    if zero_init:
        c_s[...] = jnp.zeros_like(c_s)
        n_s[...] = jnp.zeros_like(n_s)
        m_init = jnp.zeros((1, 1), F32)
    else:
        c_s[...] = c0_ref[0, 0, 0]
        n_s[...] = n0_ref[0, 0, 0]
        m_init = m0_ref[0, 0, 0][:, 0:1]

    def step(ci, m_prev):
        chunk = ci + dr * (nc - 1 - 2 * ci)
        rows = pl.ds(pl.multiple_of(chunk * t, t), t)
        qc = q_ref[0, rows, :]
        kc = k_ref[0, rows, :]
        vc = v_ref[0, rows, :]
        gates = g_ref[0, rows, :] + bg_ref[...]
        hi, mid, lo = _split3(_log_sigmoid(gates))
        bsum = _dot(tri, hi) + _dot(tri, mid) + _dot(tri, lo)
        b_col = _pick_col(bsum, col_f)
        i_col = _pick_col(gates, col_i)
        b_row = _pick_row(bsum.T, col_f)
        i_row = _pick_row(gates.T, col_i)

        a_inter = b_col + m_prev
        dmat = jnp.where(mask, b_col - b_row + i_row, -jnp.inf)
        m_t = jnp.maximum(a_inter, jnp.max(dmat, axis=1, keepdims=True))
        w_inter = jnp.exp(a_inter - m_t)
        qk = lax.dot_general(qc, kc, (((1,), (1,)), ((), ())), preferred_element_type=F32)
        s = jnp.where(mask, qk * jnp.exp(dmat - m_t), 0.0)
        c_prev = c_s[...]
        n_prev = n_s[...]
        num = _dot(s.astype(BF16), vc) + w_inter * _dot(qc, c_prev.astype(BF16))
        den = (jnp.sum(s, axis=1, keepdims=True)
               + w_inter * jnp.sum(qc.astype(F32) * n_prev, axis=1, keepdims=True))
        h_ref[0, 0, rows, :] = num / jnp.maximum(jnp.abs(den), jnp.exp(-m_t))

        lane = lax.broadcasted_iota(jnp.int32, (1, t), 1)
        b_end = jnp.sum(jnp.where(lane == end_lane, b_row, 0.0), axis=1, keepdims=True)
        g_row = b_end - b_row + i_row
        m_new = jnp.maximum(b_end + m_prev, jnp.max(g_row, axis=1, keepdims=True))
        decay = jnp.exp(b_end + m_prev - m_new)
        ws_col = jnp.exp(b_end - b_col + i_col - m_new)
        kw = ws_col * kc.astype(F32)
        c_s[...] = decay * c_prev + _dot(kw.T.astype(BF16), vc)
        n_s[...] = decay * n_prev + jnp.sum(kw, axis=0, keepdims=True)
        return m_new

    m_fin = lax.fori_loop(0, nc, step, m_init)
    co_ref[0, 0, 0] = c_s[...]
    no_ref[0, 0, 0] = n_s[...]
    mo_ref[0, 0, 0] = jnp.broadcast_to(m_fin, mo_ref.shape[3:])


def _mlstm_scan(q, k, v, z, gate_blk, bg, init):
    b, l, ci = q.shape
    hd = ci // C_HEADS
    zero_init = init is None
    qkv = pl.BlockSpec((1, l, hd), lambda bi, h, dr: (bi, 0, h))
    st = lambda *tail: pl.BlockSpec((1, 1, 1) + tail, lambda bi, h, dr: (bi, dr, h, 0, 0))
    in_specs = [qkv, qkv, qkv,
                pl.BlockSpec((1, l, 128), lambda bi, h, dr: (bi, 0, gate_blk)),
                pl.BlockSpec((1, 128), lambda bi, h, dr: (0, 0))]
    args = [q, k, v, z, bg]
    if not zero_init:
        in_specs += [st(hd, hd), st(1, hd), st(1, 128)]
        args += list(init)
    return pl.pallas_call(
        functools.partial(_mlstm_scan_body, zero_init=zero_init),
        grid=(b, C_HEADS, 2),
        in_specs=in_specs,
        out_specs=[pl.BlockSpec((1, 1, l, hd), lambda bi, h, dr: (dr, bi, 0, h)),
                   st(hd, hd), st(1, hd), st(1, 128)],
        out_shape=[jax.ShapeDtypeStruct((2, b, l, ci), F32),
                   jax.ShapeDtypeStruct((b, 2, C_HEADS, hd, hd), F32),
                   jax.ShapeDtypeStruct((b, 2, C_HEADS, 1, hd), F32),
                   jax.ShapeDtypeStruct((b, 2, C_HEADS, 1, 128), F32)],
        scratch_shapes=[pltpu.VMEM((hd, hd), F32), pltpu.VMEM((1, hd), F32)],
        compiler_params=_cparams("parallel", "parallel", "arbitrary"),
        name="mlstm_scan",
    )(*args)


def _out_proj1_body(hf_ref, hb_ref, og_ref, xc_ref, hn_ref, sk_ref, wo_ref, x_ref, mod_ref, o_ref):
    ci = og_ref.shape[1]
    hd = ci // C_HEADS
    hs = hf_ref[0] + hb_ref[0]
    hn = jnp.concatenate([_rms(hs[:, h * hd:(h + 1) * hd], hn_ref[:, h * hd:(h + 1) * hd]) for h in range(C_HEADS)],
                         axis=1)
    y = _sigmoid(og_ref[...]) * (hn + sk_ref[...] * xc_ref[...])
    o_ref[...] = x_ref[...] + mod_ref[0, 2:3, :] * _dot(y.astype(BF16), wo_ref[...])


def _out_proj1(h2, z, xc, hnw, skip, wo, x, mod, rows_per_mod, tm):
    m, d = x.shape
    ci = xc.shape[1]
    const = lambda i: (0, 0)
    return pl.pallas_call(
        _out_proj1_body,
        grid=(m // tm,),
        in_specs=[pl.BlockSpec((1, tm, ci), lambda i: (0, i, 0)),
                  pl.BlockSpec((1, tm, ci), lambda i: (1, i, 0)),
                  pl.BlockSpec((tm, ci), lambda i: (i, 1)),
                  pl.BlockSpec((tm, ci), lambda i: (i, 0)),
                  pl.BlockSpec((1, ci), const),
                  pl.BlockSpec((1, ci), const),
                  pl.BlockSpec(wo.shape, const),
                  pl.BlockSpec((tm, d), lambda i: (i, 0)),
                  pl.BlockSpec((1, N_MOD, d), lambda i: ((i * tm) // rows_per_mod, 0, 0))],
        out_specs=pl.BlockSpec((tm, d), lambda i: (i, 0)),
        out_shape=jax.ShapeDtypeStruct((m, d), F32),
        compiler_params=_cparams("parallel"),
        name="mlstm_out_proj",
    )(h2, h2, z, xc, hnw.reshape(1, ci), skip.reshape(1, ci), wo, x, mod)


def _rope_tables(length):
    n_rows = length // GRID_W
    rows = jnp.repeat(jnp.arange(n_rows, dtype=F32), GRID_W)
    cols = jnp.tile(jnp.arange(GRID_W, dtype=F32), n_rows)
    inv = ROPE_BASE ** (-jnp.arange(N_FREQ, dtype=F32) / N_FREQ)
    ang = jnp.stack([rows[:, None] * inv, cols[:, None] * inv], axis=1)
    cos, sin = jnp.cos(ang), jnp.sin(ang)
    zeros = jnp.zeros_like(sin)
    lanes = lambda first, second: jnp.tile(
        jnp.stack([first, second], axis=2).reshape(length, A_QK), (1, A_WIDTH // A_QK))
    return lanes(cos, cos), lanes(-sin, zeros), lanes(zeros, sin)


def _pick_tile(n, want):
    t = min(n, want)
    while n % t:
        t //= 2
    return t


def _run_group(x3, mods, ctx, init, p, tiles):
    b, l, d = x3.shape
    m = b * l
    rows_per_mod = l if mods.shape[1] > 1 else m
    tm = _pick_tile(l if mods.shape[1] > 1 else m, tiles["tm"])
    x = x3.reshape(m, d)
    extras = {}
    depth = mods.shape[0]
    for layer in range(depth):
        mod = mods[layer]
        if layer % 2 == 0:
            e = layer // 2
            lam_init = 0.8 - 0.6 * math.exp(-0.3 * layer)
            z = _in_proj(x, p["norm1_w"][layer], mod, p["w_in0"][e], rows_per_mod, tm)
            z3 = z.reshape(b, l, -1)
            rope = _rope_tables(l) if ctx is not None else None
            q, kt, v = _qkv_prep(z3, rope, _pick_tile(l, 512))
            ctx_e = _ctx_prep(ctx[0][:, e].reshape(b, -1, A_WIDTH), ctx[1][:, e].reshape(b, -1, A_WIDTH)) \
                if ctx is not None else None
            lam4 = jnp.stack([p["lam_q1"][e], p["lam_k1"][e], p["lam_q2"][e], p["lam_k2"][e]])
            tk = _pick_tile(math.gcd(l, ctx[0].shape[2]) if ctx is not None else l, tiles["tk"])
            a = _attention(q, kt, v, ctx_e, lam4, p["subln_w"][e], lam_init, _pick_tile(l, tiles["tq"]), tk)
            x = _out_proj0(a.reshape(m, A_WIDTH), z, p["gate_norm_w"][e], p["w_spatial"][e],
                           p["b_spatial"][e].T, p["w_out0"][e], x, mod, rows_per_mod, tm)
            extras.setdefault("k", []).append(z3[:, :, A_WIDTH:2 * A_WIDTH].reshape(b, l, A_HEADS, A_V))
            extras.setdefault("v", []).append(z3[:, :, 2 * A_WIDTH:3 * A_WIDTH].reshape(b, l, A_HEADS, A_V))
        else:
            o = layer // 2
            ci = p["w_out1"].shape[1]
            z = _in_proj(x, p["norm1_w"][layer], mod, p["w_in1"][o], rows_per_mod, tm)
            q, k, v, xc = _mlstm_qkv(z, ci, p["mconv_w"][o], p["mconv_b"][o], p["w_q"][o], p["w_k"][o], p["w_v"][o],
                                     l, tm)
            init_o = None if init is None else tuple(s[:, o] for s in init)
            h2, c_fin, n_fin, m_fin = _mlstm_scan(q.reshape(b, l, ci), k.reshape(b, l, ci), v.reshape(b, l, ci),
                                                  z.reshape(b, l, -1), 2 * ci // 128, p["b_gates"][o], init_o)
            x = _out_proj1(h2.reshape(2, m, ci), z, xc, p["head_norm_w"][o], p["skip_w"][o], p["w_out1"][o],
                           x, mod, rows_per_mod, tm)
            extras.setdefault("C", []).append(c_fin)
            extras.setdefault("n", []).append(n_fin[:, :, :, 0, :])
            extras.setdefault("m", []).append(m_fin[:, :, :, 0, 0])
        fw = p["final_norm_w"] if layer == depth - 1 else None
        x = _conv_ffn(x, p["norm2_w"][layer], mod, p["w_up_a"][layer], p["w_up_g"][layer], p["fconv_w"][layer],
                      p["fconv_b"][layer], p["w_down"][layer], fw, rows_per_mod, l, tm, tiles["fc"])
    return x.reshape(b, l, d), extras


def kernel(x_prompt, x_sample, cache_k, cache_v, state_C, state_n, state_m, c, c_ctx, w_mod, b_mod, norm1_w, norm2_w,
           w_in0, lam_q1, lam_k1, lam_q2, lam_k2, subln_w, gate_norm_w, w_spatial, b_spatial, w_out0, w_in1,
           b_gates, mconv_w, mconv_b, w_q, w_k, w_v, head_norm_w, skip_w, w_out1, w_up, fconv_w, fconv_b, w_down,
           final_norm_w):
    depth, d, _ = w_mod.shape
    dec_b = x_sample.shape[0]
    dff = w_down.shape[1]
    ci = w_out1.shape[1]

    n_cond = -(-(dec_b + 1) // 8) * 8
    cond = jnp.zeros((n_cond, d), F32).at[:dec_b].set(c).at[dec_b].set(c_ctx)
    mods = _modulation(cond, w_mod, b_mod).reshape(depth, n_cond, N_MOD, d)

    n_in1 = w_in1.shape[2]
    pad1 = -(-n_in1 // 128) * 128 - n_in1
    p = dict(
        norm1_w=norm1_w, norm2_w=norm2_w, lam_q1=lam_q1, lam_k1=lam_k1, lam_q2=lam_q2, lam_k2=lam_k2,
        subln_w=subln_w, gate_norm_w=gate_norm_w, b_spatial=b_spatial, mconv_w=mconv_w, mconv_b=mconv_b,
        head_norm_w=head_norm_w, skip_w=skip_w, fconv_w=fconv_w, fconv_b=fconv_b, final_norm_w=final_norm_w,
        w_in0=w_in0.astype(BF16), w_spatial=w_spatial.astype(BF16), w_out0=w_out0.astype(BF16),
        w_in1=jnp.pad(w_in1, ((0, 0), (0, 0), (0, pad1))).astype(BF16),
        b_gates=jnp.pad(b_gates, ((0, 0), (0, 128 - b_gates.shape[1])))[:, None, :],
        w_q=w_q.astype(BF16), w_k=w_k.astype(BF16), w_v=w_v.astype(BF16), w_out1=w_out1.astype(BF16),
        w_up_a=w_up[:, :, :dff].astype(BF16), w_up_g=w_up[:, :, dff:].astype(BF16), w_down=w_down.astype(BF16),
    )
    tiles = dict(tm=512, tq=256, tk=512, fc=256)

    y_prompt, ex = _run_group(x_prompt, mods[:, dec_b:dec_b + 1], None, None, p, tiles)
    init = (state_C.astype(F32),
            state_n.astype(F32)[:, :, :, :, None, :],
            jnp.broadcast_to(state_m.astype(F32)[:, :, :, :, None, None], state_m.shape + (1, 128)))
    y_sample, _ = _run_group(x_sample, mods[:, :dec_b], (cache_k, cache_v), init, p, tiles)

    return (y_prompt, y_sample, jnp.stack(ex["k"], axis=1), jnp.stack(ex["v"], axis=1),
            jnp.stack(ex["C"], axis=1), jnp.stack(ex["n"], axis=1), jnp.stack(ex["m"], axis=1))
```

```python
import functools
import math

import jax
import jax.numpy as jnp
from jax import lax
from jax.experimental import pallas as pl
from jax.experimental.pallas import tpu as pltpu

F32 = jnp.float32
BF16 = jnp.bfloat16

EPS = 1e-6
ROPE_BASE = 10000.0
GRID_W = 64
N_MOD = 6
A_HEADS = 4
A_QK = 64
A_V = 2 * A_QK
A_WIDTH = A_HEADS * A_V
N_FREQ = A_QK // 4
B_GROUPS = 4
CHUNK = 128
C_HEADS = 4
LOG2E = 1.4426950408889634

HALO = 8
VMEM_LIMIT = 56 * 1024 * 1024
NEG_BIG = -1e30


def _cparams(*sem):
    return pltpu.CompilerParams(dimension_semantics=sem, vmem_limit_bytes=VMEM_LIMIT)


def _gelu(x):
    return 0.5 * x * (1.0 + jnp.tanh(math.sqrt(2.0 / math.pi) * (x + 0.044715 * (x * x * x))))


def _sigmoid(x):
    return 1.0 / (1.0 + jnp.exp(-x))


def _log_sigmoid(x):
    return jnp.minimum(x, 0.0) - jnp.log1p(jnp.exp(-jnp.abs(x)))


def _rms(x, w):
    return x * lax.rsqrt(jnp.mean(x * x, axis=-1, keepdims=True) + EPS) * w


def _modulated(x, nw, shift, scale):
    return _rms(x, nw) * (1.0 + scale) + shift


def _dot(a, b):
    return jnp.dot(a, b, preferred_element_type=F32)


def _conv3(xe, tm, seq_len, row0, w_ref, b_ref):
    n = xe.shape[0]
    prev = pltpu.roll(xe, 1, 0)[HALO:HALO + tm]
    nxt = pltpu.roll(xe, n - 1, 0)[HALO:HALO + tm]
    cur = xe[HALO:HALO + tm]
    pos = (row0 + lax.broadcasted_iota(jnp.int32, (tm, 1), 0)) % seq_len
    prev = jnp.where(pos == 0, 0.0, prev)
    nxt = jnp.where(pos == seq_len - 1, 0.0, nxt)
    return prev * w_ref[0:1, :] + cur * w_ref[1:2, :] + nxt * w_ref[2:3, :] + b_ref[...]


def _mod_body(c_ref, w_ref, b_ref, o_ref):
    c = c_ref[...]
    s = (c * _sigmoid(c)).astype(BF16)
    o_ref[0] = _dot(s, w_ref[0].astype(BF16)) + b_ref[0]


def _modulation(cond, w_mod, b_mod):
    depth, d, n = w_mod.shape
    r = cond.shape[0]
    tn = n // 4
    return pl.pallas_call(
        _mod_body,
        grid=(depth, n // tn),
        in_specs=[pl.BlockSpec((r, d), lambda l, j: (0, 0)),
                  pl.BlockSpec((1, d, tn), lambda l, j: (l, 0, j)),
                  pl.BlockSpec((1, 1, tn), lambda l, j: (l, 0, j))],
        out_specs=pl.BlockSpec((1, r, tn), lambda l, j: (l, 0, j)),
        out_shape=jax.ShapeDtypeStruct((depth, r, n), F32),
        compiler_params=_cparams("parallel", "parallel"),
        name="modulation",
    )(cond, w_mod, b_mod.reshape(depth, 1, n))


def _in_proj_body(x_ref, nw_ref, mod_ref, w_ref, o_ref):
    h = _modulated(x_ref[...], nw_ref[...], mod_ref[0, 0:1, :], mod_ref[0, 1:2, :])
    o_ref[...] = _dot(h.astype(BF16), w_ref[...]).astype(o_ref.dtype)


def _in_proj(x, nw, mod, w, rows_per_mod, tm):
    m, d = x.shape
    n = w.shape[1]
    return pl.pallas_call(
        _in_proj_body,
        grid=(m // tm,),
        in_specs=[pl.BlockSpec((tm, d), lambda i: (i, 0)),
                  pl.BlockSpec((1, d), lambda i: (0, 0)),
                  pl.BlockSpec((1, N_MOD, d), lambda i: ((i * tm) // rows_per_mod, 0, 0)),
                  pl.BlockSpec((d, n), lambda i: (0, 0))],
        out_specs=pl.BlockSpec((tm, n), lambda i: (i, 0)),
        out_shape=jax.ShapeDtypeStruct((m, n), F32),
        compiler_params=_cparams("parallel"),
        name="in_proj",
    )(x, nw.reshape(1, d), mod, w)


def _qkv_prep_body(*refs, rope):
    if rope:
        q_ref, k_ref, v_ref, cos_ref, sa_ref, sb_ref, qo_ref, kt_ref, vo_ref = refs
    else:
        q_ref, k_ref, v_ref, qo_ref, kt_ref, vo_ref = refs
    q = q_ref[0]
    k = k_ref[0]
    if rope:
        w = q.shape[-1]

        def rot(x):
            return (x * cos_ref[...] + pltpu.roll(x, w - N_FREQ, 1) * sa_ref[...]
                    + pltpu.roll(x, N_FREQ, 1) * sb_ref[...])

        q = rot(q)
        k = rot(k)
    qo_ref[0] = (q * (A_QK ** -0.5 * LOG2E)).astype(BF16)
    kt_ref[0] = k.T.astype(BF16)
    vo_ref[0] = v_ref[0].astype(BF16)


def _qkv_prep(z, rope_tabs, tl):
    b, l, _ = z.shape
    w = A_WIDTH
    rope = rope_tabs is not None
    in_specs = [pl.BlockSpec((1, tl, w), lambda bi, j: (bi, j, 0)),
                pl.BlockSpec((1, tl, w), lambda bi, j: (bi, j, 1)),
                pl.BlockSpec((1, tl, w), lambda bi, j: (bi, j, 2))]
    args = [z, z, z]
    if rope:
        in_specs += [pl.BlockSpec((tl, w), lambda bi, j: (j, 0))] * 3
        args += list(rope_tabs)
    return pl.pallas_call(
        functools.partial(_qkv_prep_body, rope=rope),
        grid=(b, l // tl),
        in_specs=in_specs,
        out_specs=[pl.BlockSpec((1, tl, w), lambda bi, j: (bi, j, 0)),
                   pl.BlockSpec((1, w, tl), lambda bi, j: (bi, 0, j)),
                   pl.BlockSpec((1, tl, w), lambda bi, j: (bi, j, 0))],
        out_shape=[jax.ShapeDtypeStruct((b, l, w), BF16),
                   jax.ShapeDtypeStruct((b, w, l), BF16),
                   jax.ShapeDtypeStruct((b, l, w), BF16)],
        compiler_params=_cparams("parallel", "parallel"),
        name="qkv_prep",
    )(*args)


def _ctx_prep_body(k_ref, v_ref, kt_ref, vo_ref):
    kt_ref[0] = k_ref[0].T.astype(BF16)
    vo_ref[0] = v_ref[0].astype(BF16)


def _ctx_prep(ctx_k, ctx_v):
    b, p, w = ctx_k.shape
    return pl.pallas_call(
        _ctx_prep_body,
        grid=(b,),
        in_specs=[pl.BlockSpec((1, p, w), lambda bi: (bi, 0, 0))] * 2,
        out_specs=[pl.BlockSpec((1, w, p), lambda bi: (bi, 0, 0)),
                   pl.BlockSpec((1, p, w), lambda bi: (bi, 0, 0))],
        out_shape=[jax.ShapeDtypeStruct((b, w, p), BF16), jax.ShapeDtypeStruct((b, p, w), BF16)],
        compiler_params=_cparams("parallel"),
        name="ctx_prep",
    )(ctx_k, ctx_v)


def _attn_body(*refs, has_ctx, tk, lam_init):
    if has_ctx:
        q_ref, kt_ref, v_ref, ckt_ref, cv_ref, lam_ref, sw_ref, o_ref = refs
    else:
        q_ref, kt_ref, v_ref, lam_ref, sw_ref, o_ref = refs
    q = q_ref[0]
    tq = q.shape[0]
    lane = lax.broadcasted_iota(jnp.int32, q.shape, 1)
    zero = jnp.zeros_like(q)
    qq = jnp.concatenate([jnp.where(lane < A_QK, q, zero), jnp.where(lane >= A_QK, q, zero)], axis=0)

    m = jnp.full((2 * tq, 1), NEG_BIG, F32)
    l = jnp.zeros((2 * tq, 1), F32)
    acc = jnp.zeros((2 * tq, A_V), F32)

    sources = ([(ckt_ref, cv_ref)] if has_ctx else []) + [(kt_ref, v_ref)]
    for k_src, v_src in sources:
        lk = k_src.shape[-1]
        ck = _pick_tile(lk, tk)
        for c in range(lk // ck):
            kt = k_src[0, :, c * ck:(c + 1) * ck]
            vv = v_src[0, c * ck:(c + 1) * ck, :]
            s = _dot(qq, kt)
            m_new = jnp.maximum(m, jnp.max(s, axis=-1, keepdims=True))
            alpha = jnp.exp2(m - m_new)
            p = jnp.exp2(s - m_new)
            l = alpha * l + jnp.sum(p, axis=-1, keepdims=True)
            acc = alpha * acc + _dot(p.astype(BF16), vv)
            m = m_new

    o = acc / l
    lam = (jnp.exp(jnp.sum(lam_ref[0:1, :] * lam_ref[1:2, :], axis=-1, keepdims=True))
           - jnp.exp(jnp.sum(lam_ref[2:3, :] * lam_ref[3:4, :], axis=-1, keepdims=True)) + lam_init)
    a = o[:tq] - lam * o[tq:]
    o_ref[0] = (_rms(a, sw_ref[...]) * (1.0 - lam_init)).astype(o_ref.dtype)


def _attention(q, kt, v, ctx, lam4, subln_w, lam_init, tq, tk):
    b, l, w = q.shape
    lk = kt.shape[-1]
    has_ctx = ctx is not None
    in_specs = [pl.BlockSpec((1, tq, A_V), lambda bi, h, i: (bi, i, h)),
                pl.BlockSpec((1, A_V, lk), lambda bi, h, i: (bi, h, 0)),
                pl.BlockSpec((1, lk, A_V), lambda bi, h, i: (bi, 0, h))]
    args = [q, kt, v]
    if has_ctx:
        p = ctx[0].shape[-1]
        in_specs += [pl.BlockSpec((1, A_V, p), lambda bi, h, i: (bi, h, 0)),
                     pl.BlockSpec((1, p, A_V), lambda bi, h, i: (bi, 0, h))]
        args += list(ctx)
    in_specs += [pl.BlockSpec((4, A_QK), lambda bi, h, i: (0, 0)),
                 pl.BlockSpec((1, A_V), lambda bi, h, i: (0, 0))]
    args += [lam4, subln_w.reshape(1, A_V)]
    return pl.pallas_call(
        functools.partial(_attn_body, has_ctx=has_ctx, tk=tk, lam_init=lam_init),
        grid=(b, A_HEADS, l // tq),
        in_specs=in_specs,
        out_specs=pl.BlockSpec((1, tq, A_V), lambda bi, h, i: (bi, i, h)),
        out_shape=jax.ShapeDtypeStruct((b, l, w), BF16),
        compiler_params=_cparams("parallel", "parallel", "parallel"),
        name="diff_attention",
    )(*args)


def _out_proj0_body(a_ref, gu_ref, gv_ref, gnw_ref, ws_ref, bs_ref, wo_ref, x_ref, mod_ref, o_ref):
    tm = x_ref.shape[0]
    bw = gu_ref.shape[1]
    ch = bw // B_GROUPS
    nch = tm // CHUNK
    u = _gelu(gu_ref[...])
    vn = _rms(_gelu(gv_ref[...]), gnw_ref[...]).astype(BF16)
    parts = []
    for g in range(B_GROUPS):
        rhs = jnp.concatenate([vn[r * CHUNK:(r + 1) * CHUNK, g * ch:(g + 1) * ch] for r in range(nch)], axis=1)
        parts.append(_dot(ws_ref[g], rhs) + bs_ref[:, g:g + 1])
    s = jnp.concatenate(
        [jnp.concatenate([parts[g][:, r * ch:(r + 1) * ch] for g in range(B_GROUPS)], axis=1) for r in range(nch)],
        axis=0)
    gg = (u * s).astype(BF16)
    aw = a_ref.shape[1]
    y = _dot(a_ref[...], wo_ref[0:aw, :]) + _dot(gg, wo_ref[aw:, :])
    o_ref[...] = x_ref[...] + mod_ref[0, 2:3, :] * y


def _out_proj0(a, z, gnw, ws, bs_t, wo, x, mod, rows_per_mod, tm):
    m, d = x.shape
    aw = a.shape[1]
    bw = gnw.shape[0]
    ub = aw * 3 // bw
    return pl.pallas_call(
        _out_proj0_body,
        grid=(m // tm,),
        in_specs=[pl.BlockSpec((tm, aw), lambda i: (i, 0)),
                  pl.BlockSpec((tm, bw), lambda i: (i, ub)),
                  pl.BlockSpec((tm, bw), lambda i: (i, ub + 1)),
                  pl.BlockSpec((1, bw), lambda i: (0, 0)),
                  pl.BlockSpec(ws.shape, lambda i: (0, 0, 0)),
                  pl.BlockSpec(bs_t.shape, lambda i: (0, 0)),
                  pl.BlockSpec(wo.shape, lambda i: (0, 0)),
                  pl.BlockSpec((tm, d), lambda i: (i, 0)),
                  pl.BlockSpec((1, N_MOD, d), lambda i: ((i * tm) // rows_per_mod, 0, 0))],
        out_specs=pl.BlockSpec((tm, d), lambda i: (i, 0)),
        out_shape=jax.ShapeDtypeStruct((m, d), F32),
        compiler_params=_cparams("parallel"),
        name="gmlp_out_proj",
    )(a, z, z, gnw.reshape(1, bw), ws, bs_t, wo, x, mod)


def _ffn_body(*refs, seq_len, fc, final):
    if final:
        x_ref, xp_ref, xn_ref, nw_ref, mod_ref, wa_ref, wg_ref, cw_ref, cb_ref, wd_ref, fw_ref, o_ref = refs
    else:
        x_ref, xp_ref, xn_ref, nw_ref, mod_ref, wa_ref, wg_ref, cw_ref, cb_ref, wd_ref, o_ref = refs
    tm, d = x_ref.shape
    dff = wa_ref.shape[1]
    x = x_ref[...]
    xe = jnp.concatenate([xp_ref[...], x, xn_ref[...]], axis=0)
    he = _modulated(xe, nw_ref[...], mod_ref[0, 3:4, :], mod_ref[0, 4:5, :]).astype(BF16)
    hc = he[HALO:HALO + tm]
    row0 = pl.program_id(0) * tm
    acc = jnp.zeros((tm, d), F32)
    for c in range(dff // fc):
        cs = slice(c * fc, (c + 1) * fc)
        ae = _dot(he, wa_ref[:, cs])
        g = _dot(hc, wg_ref[:, cs])
        conv = _conv3(ae, tm, seq_len, row0, cw_ref.at[:, cs], cb_ref.at[:, cs])
        acc = acc + _dot((_gelu(conv) * g).astype(BF16), wd_ref[cs, :])
    out = x + mod_ref[0, 5:6, :] * acc
    if final:
        out = _rms(out, fw_ref[...])
    o_ref[...] = out


def _halo_specs(tm, m, width, col):
    nb = tm // HALO
    last = m // HALO - 1
    return [pl.BlockSpec((HALO, width), lambda i: (jnp.maximum(i * nb - 1, 0), col)),
            pl.BlockSpec((HALO, width), lambda i: (jnp.minimum((i + 1) * nb, last), col))]


def _conv_ffn(x, nw, mod, wa, wg, cw, cb, wd, fw, rows_per_mod, seq_len, tm, fc):
    m, d = x.shape
    dff = wa.shape[1]
    final = fw is not None
    const = lambda i: (0, 0)
    in_specs = ([pl.BlockSpec((tm, d), lambda i: (i, 0))] + _halo_specs(tm, m, d, 0)
                + [pl.BlockSpec((1, d), const),
                   pl.BlockSpec((1, N_MOD, d), lambda i: ((i * tm) // rows_per_mod, 0, 0)),
                   pl.BlockSpec((d, dff), const),
                   pl.BlockSpec((d, dff), const),
                   pl.BlockSpec((3, dff), const),
                   pl.BlockSpec((1, dff), const),
                   pl.BlockSpec((dff, d), const)])
    args = [x, x, x, nw.reshape(1, d), mod, wa, wg, cw, cb.reshape(1, dff), wd]
    if final:
        in_specs.append(pl.BlockSpec((1, d), const))
        args.append(fw.reshape(1, d))
    return pl.pallas_call(
        functools.partial(_ffn_body, seq_len=seq_len, fc=fc, final=final),
        grid=(m // tm,),
        in_specs=in_specs,
        out_specs=pl.BlockSpec((tm, d), lambda i: (i, 0)),
        out_shape=jax.ShapeDtypeStruct((m, d), F32),
        compiler_params=_cparams("parallel"),
        name="conv_ffn",
    )(*args)


def _mlstm_qkv_body(xm_ref, xp_ref, xn_ref, cw_ref, cb_ref, wq_ref, wk_ref, wv_ref,
                    q_ref, kt_ref, v_ref, xc_ref, *, seq_len):
    tm, ci = xm_ref.shape
    hd = ci // C_HEADS
    xm = xm_ref[...]
    xe = jnp.concatenate([xp_ref[...], xm, xn_ref[...]], axis=0)
    conv = _conv3(xe, tm, seq_len, pl.program_id(0) * tm, cw_ref, cb_ref)
    xc = conv * _sigmoid(conv)
    xcb = xc.astype(BF16)
    xc_ref[...] = xcb
    xmb = xm.astype(BF16)
    for h in range(C_HEADS):
        hs = slice(h * hd, (h + 1) * hd)
        q_ref[:, hs] = _dot(xcb[:, hs], wq_ref[h]).astype(BF16)
        k = _dot(xcb[:, hs], wk_ref[h]) * hd ** -0.5
        for r in range(tm // CHUNK):
            kt_ref[r, hs, :] = k[r * CHUNK:(r + 1) * CHUNK].T.astype(BF16)
        v_ref[:, hs] = _dot(xmb[:, hs], wv_ref[h]).astype(BF16)


def _mlstm_qkv(z, ci, cw, cb, wq, wk, wv, seq_len, tm):
    m = z.shape[0]
    const3 = lambda i: (0, 0, 0)
    row = pl.BlockSpec((tm, ci), lambda i: (i, 0))
    return pl.pallas_call(
        functools.partial(_mlstm_qkv_body, seq_len=seq_len),
        grid=(m // tm,),
        in_specs=[row] + _halo_specs(tm, m, ci, 0)
        + [pl.BlockSpec((3, ci), lambda i: (0, 0)),
           pl.BlockSpec((1, ci), lambda i: (0, 0)),
           pl.BlockSpec(wq.shape, const3), pl.BlockSpec(wk.shape, const3), pl.BlockSpec(wv.shape, const3)],
        out_specs=[row, pl.BlockSpec((tm // CHUNK, ci, CHUNK), lambda i: (i, 0, 0)), row, row],
        out_shape=[jax.ShapeDtypeStruct((m, ci), BF16)]
        + [jax.ShapeDtypeStruct((m // CHUNK, ci, CHUNK), BF16)] + [jax.ShapeDtypeStruct((m, ci), BF16)] * 2,
        compiler_params=_cparams("parallel"),
        name="mlstm_qkv",
    )(z, z, z, cw, cb.reshape(1, ci), wq, wk, wv)


def _split3(x):
    hi = x.astype(BF16)
    r1 = x - hi.astype(F32)
    mid = r1.astype(BF16)
    lo = (r1 - mid.astype(F32)).astype(BF16)
    return hi, mid, lo


def _mlstm_scan_body(*refs, zero_init, cpb):
    it = iter(refs)
    fwd = [next(it) for _ in range(4)]
    bwd = [next(it) for _ in range(4)]
    bg_ref = next(it)
    init = None if zero_init else [next(it) for _ in range(3)]
    hf_ref, hb_ref, c_ref, n_ref, m_ref, cx_ref = [next(it) for _ in range(6)]
    t = CHUNK
    hd = c_ref.shape[-1]
    nw = cx_ref.shape[-1] - hd
    j = pl.program_id(1)

    @pl.when(j == 0)
    def _():
        for dr in range(2):
            for h in range(C_HEADS):
                if zero_init:
                    cx_ref[dr, h] = jnp.zeros(cx_ref.shape[2:], F32)
                else:
                    cx_ref[dr, h, :, 0:hd] = init[0][0, dr, h]
                    cx_ref[dr, h, :, hd:] = jnp.broadcast_to(init[1][0, dr, h], (nw, hd)).T
        m_ref[...] = jnp.zeros_like(m_ref) if zero_init else init[2][...]

    rr = lax.broadcasted_iota(jnp.int32, (t, t), 0)
    cc = lax.broadcasted_iota(jnp.int32, (t, t), 1)
    ones = jnp.ones((t, nw), BF16)

    srcs = ((fwd, hf_ref), (bwd, hb_ref))
    chunk_of = lambda ci, dr: ci if dr == 0 else cpb - 1 - ci

    m_cur = {(dr, h): m_ref[0, dr, h][:, 0:1] for dr in range(2) for h in range(C_HEADS)}
    pre = {}
    for ci in range(cpb):
        for dr in range(2):
            g_ref = srcs[dr][0][3]
            ch = chunk_of(ci, dr)
            mask = (rr >= cc) if dr == 0 else (rr <= cc)
            tri = jnp.where(mask, 1.0, 0.0).astype(BF16)
            gates = g_ref[0, ch * t:(ch + 1) * t, :] + bg_ref[...]
            hi, mid, lo = _split3(_log_sigmoid(gates))
            bsum = _dot(tri, hi) + _dot(tri, mid) + _dot(tri, lo)
            bsum_t = bsum.T
            gates_t = gates.T
            for h in range(C_HEADS):
                col_i = 2 * dr * C_HEADS + h
                col_f = (2 * dr + 1) * C_HEADS + h
                b_col = jnp.broadcast_to(bsum[:, col_f:col_f + 1], (t, t))
                b_row = bsum_t[col_f:col_f + 1, :]
                i_row = gates_t[col_i:col_i + 1, :]
                m_prev = m_cur[dr, h]
                a_inter = b_col + m_prev
                dmat = jnp.where(mask, b_col - b_row + i_row, -jnp.inf)
                m_t = jnp.maximum(a_inter, jnp.max(dmat, axis=1, keepdims=True))
                b_end = b_row[:, t - 1:t] if dr == 0 else b_row[:, 0:1]
                g_row = b_end - b_row + i_row
                m_new = jnp.maximum(b_end + m_prev, jnp.max(g_row, axis=1, keepdims=True))
                m_cur[dr, h] = m_new
                pre[ci, dr, h] = (jnp.where(mask, jnp.exp(dmat - m_t), 0.0),
                                  jnp.exp(a_inter - m_t),
                                  jnp.exp(-m_t),
                                  jnp.exp(b_end + m_prev - m_new),
                                  jnp.exp(g_row - m_new))

    for ci in range(cpb):
        for dr in range(2):
            (q_ref, kt_ref, v_ref, _), h_out = srcs[dr]
            ch = chunk_of(ci, dr)
            rows = slice(ch * t, (ch + 1) * t)
            for h in range(C_HEADS):
                decay_w, w_inter, floor, decay, w_row = pre[ci, dr, h]
                hs = slice(h * hd, (h + 1) * hd)
                cx = cx_ref[dr, h]
                qc = q_ref[0, rows, hs]
                v1 = jnp.concatenate([v_ref[0, rows, hs], ones], axis=1)
                ktc = kt_ref[ch, hs, :]
                s = _dot(qc, ktc) * decay_w
                intra = _dot(s.astype(BF16), v1)
                inter = _dot(qc, cx.astype(BF16))
                den = intra[:, hd:] + w_inter * inter[:, hd:]
                inv = 1.0 / jnp.maximum(jnp.abs(den), floor)
                reps = hd // t
                num = intra[:, 0:hd] + jnp.concatenate([w_inter] * reps, axis=1) * inter[:, 0:hd]
                h_out[0, rows, hs] = (num * jnp.concatenate([inv] * reps, axis=1)).astype(h_out.dtype)
                kw_t = (ktc.astype(F32) * w_row).astype(BF16)
                cx_ref[dr, h] = decay * cx + _dot(kw_t, v1)

    for (dr, h), m_new in m_cur.items():
        m_ref[0, dr, h] = jnp.broadcast_to(m_new, (1, m_ref.shape[-1]))

    @pl.when(j == pl.num_programs(1) - 1)
    def _():
        for dr in range(2):
            for h in range(C_HEADS):
                c_ref[0, dr, h] = cx_ref[dr, h, :, 0:hd]
                n_ref[0, dr, h] = cx_ref[dr, h, :, hd:].T[0:1, :]


def _mlstm_scan(q, kt, v, z, gate_blk, bg, init, lb):
    b, l, ci = q.shape
    hd = ci // C_HEADS
    nb = l // lb
    cpb = lb // CHUNK
    zero_init = init is None
    in_specs, args = [], []
    for blk in (lambda j: j, lambda j: nb - 1 - j):
        row = pl.BlockSpec((1, lb, ci), lambda bi, j, blk=blk: (bi, blk(j), 0))
        in_specs += [row,
                     pl.BlockSpec((cpb, ci, CHUNK), lambda bi, j, blk=blk: (bi * nb + blk(j), 0, 0)),
                     row,
                     pl.BlockSpec((1, lb, 128), lambda bi, j, blk=blk: (bi, blk(j), gate_blk))]
        args += [q, kt, v, z]
    in_specs.append(pl.BlockSpec((1, 128), lambda bi, j: (0, 0)))
    args.append(bg)
    st = lambda *tail: pl.BlockSpec((1, 2, C_HEADS) + tail, lambda bi, j: (bi, 0, 0, 0, 0))
    states = [st(hd, hd), st(1, hd), st(1, 128)]
    if not zero_init:
        in_specs += states
        args += list(init)
    return pl.pallas_call(
        functools.partial(_mlstm_scan_body, zero_init=zero_init, cpb=cpb),
        grid=(b, nb),
        in_specs=in_specs,
        out_specs=[pl.BlockSpec((1, lb, ci), lambda bi, j: (bi, j, 0)),
                   pl.BlockSpec((1, lb, ci), lambda bi, j: (bi, nb - 1 - j, 0))] + states,
        out_shape=[jax.ShapeDtypeStruct((b, l, ci), BF16),
                   jax.ShapeDtypeStruct((b, l, ci), BF16),
                   jax.ShapeDtypeStruct((b, 2, C_HEADS, hd, hd), F32),
                   jax.ShapeDtypeStruct((b, 2, C_HEADS, 1, hd), F32),
                   jax.ShapeDtypeStruct((b, 2, C_HEADS, 1, 128), F32)],
        scratch_shapes=[pltpu.VMEM((2, C_HEADS, hd, hd + CHUNK), F32)],
        compiler_params=_cparams("parallel", "arbitrary"),
        name="mlstm_scan",
    )(*args)


def _out_proj1_body(hf_ref, hb_ref, og_ref, xc_ref, hn_ref, sk_ref, wo_ref, x_ref, mod_ref, o_ref):
    ci = og_ref.shape[1]
    hd = ci // C_HEADS
    hs = hf_ref[...].astype(F32) + hb_ref[...].astype(F32)
    hn = jnp.concatenate([_rms(hs[:, h * hd:(h + 1) * hd], hn_ref[:, h * hd:(h + 1) * hd]) for h in range(C_HEADS)],
                         axis=1)
    y = _sigmoid(og_ref[...]) * (hn + sk_ref[...] * xc_ref[...].astype(F32))
    o_ref[...] = x_ref[...] + mod_ref[0, 2:3, :] * _dot(y.astype(BF16), wo_ref[...])


def _out_proj1(hf, hb, z, xc, hnw, skip, wo, x, mod, rows_per_mod, tm):
    m, d = x.shape
    ci = xc.shape[1]
    const = lambda i: (0, 0)
    row = pl.BlockSpec((tm, ci), lambda i: (i, 0))
    return pl.pallas_call(
        _out_proj1_body,
        grid=(m // tm,),
        in_specs=[row, row,
                  pl.BlockSpec((tm, ci), lambda i: (i, 1)),
                  row,
                  pl.BlockSpec((1, ci), const),
                  pl.BlockSpec((1, ci), const),
                  pl.BlockSpec(wo.shape, const),
                  pl.BlockSpec((tm, d), lambda i: (i, 0)),
                  pl.BlockSpec((1, N_MOD, d), lambda i: ((i * tm) // rows_per_mod, 0, 0))],
        out_specs=pl.BlockSpec((tm, d), lambda i: (i, 0)),
        out_shape=jax.ShapeDtypeStruct((m, d), F32),
        compiler_params=_cparams("parallel"),
        name="mlstm_out_proj",
    )(hf, hb, z, xc, hnw.reshape(1, ci), skip.reshape(1, ci), wo, x, mod)


def _rope_tables(length):
    n_rows = length // GRID_W
    rows = jnp.repeat(jnp.arange(n_rows, dtype=F32), GRID_W)
    cols = jnp.tile(jnp.arange(GRID_W, dtype=F32), n_rows)
    inv = ROPE_BASE ** (-jnp.arange(N_FREQ, dtype=F32) / N_FREQ)
    ang = jnp.stack([rows[:, None] * inv, cols[:, None] * inv], axis=1)
    cos, sin = jnp.cos(ang), jnp.sin(ang)
    zeros = jnp.zeros_like(sin)
    lanes = lambda first, second: jnp.tile(
        jnp.stack([first, second], axis=2).reshape(length, A_QK), (1, A_WIDTH // A_QK))
    return lanes(cos, cos), lanes(-sin, zeros), lanes(zeros, sin)


def _pick_tile(n, want):
    t = min(n, want)
    while n % t:
        t //= 2
    return t


def _run_group(x3, mods, ctx, init, p, tiles):
    b, l, d = x3.shape
    m = b * l
    rows_per_mod = l if mods.shape[1] > 1 else m
    tm = _pick_tile(l if mods.shape[1] > 1 else m, tiles["tm"])
    x = x3.reshape(m, d)
    extras = {}
    depth = mods.shape[0]
    for layer in range(depth):
        mod = mods[layer]
        if layer % 2 == 0:
            e = layer // 2
            lam_init = 0.8 - 0.6 * math.exp(-0.3 * layer)
            z = _in_proj(x, p["norm1_w"][layer], mod, p["w_in0"][e], rows_per_mod, tm)
            z3 = z.reshape(b, l, -1)
            rope = _rope_tables(l) if ctx is not None else None
            q, kt, v = _qkv_prep(z3, rope, _pick_tile(l, 512))
            ctx_e = _ctx_prep(ctx[0][:, e].reshape(b, -1, A_WIDTH), ctx[1][:, e].reshape(b, -1, A_WIDTH)) \
                if ctx is not None else None
            lam4 = jnp.stack([p["lam_q1"][e], p["lam_k1"][e], p["lam_q2"][e], p["lam_k2"][e]])
            a = _attention(q, kt, v, ctx_e, lam4, p["subln_w"][e], lam_init, _pick_tile(l, tiles["tq"]), tiles["tk"])
            x = _out_proj0(a.reshape(m, A_WIDTH), z, p["gate_norm_w"][e], p["w_spatial"][e],
                           p["b_spatial"][e].T, p["w_out0"][e], x, mod, rows_per_mod, tm)
            extras.setdefault("k", []).append(z3[:, :, A_WIDTH:2 * A_WIDTH].reshape(b, l, A_HEADS, A_V))
            extras.setdefault("v", []).append(z3[:, :, 2 * A_WIDTH:3 * A_WIDTH].reshape(b, l, A_HEADS, A_V))
        else:
            o = layer // 2
            ci = p["w_out1"].shape[1]
            z = _in_proj(x, p["norm1_w"][layer], mod, p["w_in1"][o], rows_per_mod, tm)
            q, kt, v, xc = _mlstm_qkv(z, ci, p["mconv_w"][o], p["mconv_b"][o], p["w_q"][o], p["w_k"][o],
                                      p["w_v"][o], l, tm)
            init_o = None if init is None else tuple(s[:, o] for s in init)
            hf, hb, c_fin, n_fin, m_fin = _mlstm_scan(q.reshape(b, l, ci), kt, v.reshape(b, l, ci),
                                                      z.reshape(b, l, -1), 2 * ci // 128, p["b_gates"][o], init_o,
                                                      _pick_tile(l, tiles["lb"]))
            x = _out_proj1(hf.reshape(m, ci), hb.reshape(m, ci), z, xc, p["head_norm_w"][o], p["skip_w"][o],
                           p["w_out1"][o], x, mod, rows_per_mod, tm)
            extras.setdefault("C", []).append(c_fin)
            extras.setdefault("n", []).append(n_fin[:, :, :, 0, :])
            extras.setdefault("m", []).append(m_fin[:, :, :, 0, 0])
        fw = p["final_norm_w"] if layer == depth - 1 else None
        x = _conv_ffn(x, p["norm2_w"][layer], mod, p["w_up_a"][layer], p["w_up_g"][layer], p["fconv_w"][layer],
                      p["fconv_b"][layer], p["w_down"][layer], fw, rows_per_mod, l, tm, tiles["fc"])
    return x.reshape(b, l, d), extras


def kernel(x_prompt, x_sample, cache_k, cache_v, state_C, state_n, state_m, c, c_ctx, w_mod, b_mod, norm1_w, norm2_w,
           w_in0, lam_q1, lam_k1, lam_q2, lam_k2, subln_w, gate_norm_w, w_spatial, b_spatial, w_out0, w_in1,
           b_gates, mconv_w, mconv_b, w_q, w_k, w_v, head_norm_w, skip_w, w_out1, w_up, fconv_w, fconv_b, w_down,
           final_norm_w):
    depth, d, _ = w_mod.shape
    dec_b = x_sample.shape[0]
    dff = w_down.shape[1]
    ci = w_out1.shape[1]

    n_cond = -(-(dec_b + 1) // 8) * 8
    cond = jnp.zeros((n_cond, d), F32).at[:dec_b].set(c).at[dec_b].set(c_ctx)
    mods = _modulation(cond, w_mod, b_mod).reshape(depth, n_cond, N_MOD, d)

    n_in1 = w_in1.shape[2]
    pad1 = -(-n_in1 // 128) * 128 - n_in1
    p = dict(
        norm1_w=norm1_w, norm2_w=norm2_w, lam_q1=lam_q1, lam_k1=lam_k1, lam_q2=lam_q2, lam_k2=lam_k2,
        subln_w=subln_w, gate_norm_w=gate_norm_w, b_spatial=b_spatial, mconv_w=mconv_w, mconv_b=mconv_b,
        head_norm_w=head_norm_w, skip_w=skip_w, fconv_w=fconv_w, fconv_b=fconv_b, final_norm_w=final_norm_w,
        w_in0=w_in0.astype(BF16), w_spatial=w_spatial.astype(BF16), w_out0=w_out0.astype(BF16),
        w_in1=jnp.pad(w_in1, ((0, 0), (0, 0), (0, pad1))).astype(BF16),
        b_gates=jnp.pad(b_gates, ((0, 0), (0, 128 - b_gates.shape[1])))[:, None, :],
        w_q=w_q.astype(BF16), w_k=w_k.astype(BF16), w_v=w_v.astype(BF16), w_out1=w_out1.astype(BF16),
        w_up_a=w_up[:, :, :dff].astype(BF16), w_up_g=w_up[:, :, dff:].astype(BF16), w_down=w_down.astype(BF16),
    )
    tiles = dict(tm=512, tq=1024, tk=1024, fc=2816, lb=512)

    y_prompt, ex = _run_group(x_prompt, mods[:, dec_b:dec_b + 1], None, None, p, tiles)
    init = (state_C.astype(F32),
            state_n.astype(F32)[:, :, :, :, None, :],
            jnp.broadcast_to(state_m.astype(F32)[:, :, :, :, None, None], state_m.shape + (1, 128)))
    y_sample, _ = _run_group(x_sample, mods[:, :dec_b], (cache_k, cache_v), init, p, tiles)

    return (y_prompt, y_sample, jnp.stack(ex["k"], axis=1), jnp.stack(ex["v"], axis=1),
            jnp.stack(ex["C"], axis=1), jnp.stack(ex["n"], axis=1), jnp.stack(ex["m"], axis=1))
```

```python
import functools
import math

import jax
import jax.numpy as jnp
from jax import lax
from jax.experimental import pallas as pl
from jax.experimental.pallas import tpu as pltpu

F32 = jnp.float32
BF16 = jnp.bfloat16

EPS = 1e-6
ROPE_BASE = 10000.0
GRID_W = 64
N_MOD = 6
A_HEADS = 4
A_QK = 64
A_V = 2 * A_QK
A_WIDTH = A_HEADS * A_V
N_FREQ = A_QK // 4
B_GROUPS = 4
CHUNK = 128
C_HEADS = 4
LOG2E = 1.4426950408889634

HALO = 8
VMEM_LIMIT = 56 * 1024 * 1024
NEG_BIG = -1e30


def _cparams(*sem):
    return pltpu.CompilerParams(dimension_semantics=sem, vmem_limit_bytes=VMEM_LIMIT)


def _gelu(x):
    return 0.5 * x * (1.0 + jnp.tanh(math.sqrt(2.0 / math.pi) * (x + 0.044715 * (x * x * x))))


def _sigmoid(x):
    return 1.0 / (1.0 + jnp.exp(-x))


def _log_sigmoid(x):
    return jnp.minimum(x, 0.0) - jnp.log1p(jnp.exp(-jnp.abs(x)))


def _rms(x, w):
    return x * lax.rsqrt(jnp.mean(x * x, axis=-1, keepdims=True) + EPS) * w


def _modulated(x, nw, shift, scale):
    return _rms(x, nw) * (1.0 + scale) + shift


def _dot(a, b):
    return jnp.dot(a, b, preferred_element_type=F32)


def _conv3(xe, tm, seq_len, row0, w_ref, b_ref):
    n = xe.shape[0]
    prev = pltpu.roll(xe, 1, 0)[HALO:HALO + tm]
    nxt = pltpu.roll(xe, n - 1, 0)[HALO:HALO + tm]
    cur = xe[HALO:HALO + tm]
    pos = (row0 + lax.broadcasted_iota(jnp.int32, (tm, 1), 0)) % seq_len
    prev = jnp.where(pos == 0, 0.0, prev)
    nxt = jnp.where(pos == seq_len - 1, 0.0, nxt)
    return prev * w_ref[0:1, :] + cur * w_ref[1:2, :] + nxt * w_ref[2:3, :] + b_ref[...]


def _mod_body(c_ref, w_ref, b_ref, o_ref):
    c = c_ref[...]
    s = (c * _sigmoid(c)).astype(BF16)
    o_ref[0] = _dot(s, w_ref[0].astype(BF16)) + b_ref[0]


def _modulation(cond, w_mod, b_mod):
    depth, d, n = w_mod.shape
    r = cond.shape[0]
    tn = n // 4
    return pl.pallas_call(
        _mod_body,
        grid=(depth, n // tn),
        in_specs=[pl.BlockSpec((r, d), lambda l, j: (0, 0)),
                  pl.BlockSpec((1, d, tn), lambda l, j: (l, 0, j)),
                  pl.BlockSpec((1, 1, tn), lambda l, j: (l, 0, j))],
        out_specs=pl.BlockSpec((1, r, tn), lambda l, j: (l, 0, j)),
        out_shape=jax.ShapeDtypeStruct((depth, r, n), F32),
        compiler_params=_cparams("parallel", "parallel"),
        name="modulation",
    )(cond, w_mod, b_mod.reshape(depth, 1, n))


def _transposed_chunks(x, o_ref):
    for r in range(x.shape[0] // CHUNK):
        o_ref[r] = x[r * CHUNK:(r + 1) * CHUNK].T.astype(o_ref.dtype)


def _in_proj0_body(*refs, rope, emit_kv):
    it = iter(refs)
    x_ref, nw_ref, mod_ref, w_ref = [next(it) for _ in range(4)]
    tabs = [next(it) for _ in range(3)] if rope else None
    q_ref, kt_ref, v_ref, g_ref = [next(it) for _ in range(4)]
    h = _modulated(x_ref[...], nw_ref[...], mod_ref[0, 0:1, :], mod_ref[0, 1:2, :])
    z = _dot(h.astype(BF16), w_ref[...])
    w = q_ref.shape[1]
    q, k, v = z[:, 0:w], z[:, w:2 * w], z[:, 2 * w:3 * w]
    if emit_kv:
        next(it)[...] = k
        next(it)[...] = v
    if rope:
        reps = w // tabs[0].shape[1]
        cos, sa, sb = [jnp.concatenate([tab[...]] * reps, axis=1) for tab in tabs]

        def rot(x):
            return x * cos + pltpu.roll(x, w - N_FREQ, 1) * sa + pltpu.roll(x, N_FREQ, 1) * sb

        q, k = rot(q), rot(k)
    q_ref[...] = (q * (A_QK ** -0.5 * LOG2E)).astype(BF16)
    _transposed_chunks(k, kt_ref)
    v_ref[...] = v.astype(BF16)
    g_ref[...] = z[:, 3 * w:].astype(g_ref.dtype)


def _in_proj0(x, nw, mod, w_in, rope_tabs, emit_kv, rows_per_mod, seq_len, tm):
    m, d = x.shape
    n = w_in.shape[1]
    w = A_WIDTH
    rope = rope_tabs is not None
    row = lambda width: pl.BlockSpec((tm, width), lambda i: (i, 0))
    in_specs = [row(d),
                pl.BlockSpec((1, d), lambda i: (0, 0)),
                pl.BlockSpec((1, N_MOD, d), lambda i: ((i * tm) // rows_per_mod, 0, 0)),
                pl.BlockSpec((d, n), lambda i: (0, 0))]
    args = [x, nw.reshape(1, d), mod, w_in]
    if rope:
        blocks = seq_len // tm
        in_specs += [pl.BlockSpec((tm, rope_tabs[0].shape[1]), lambda i: (i % blocks, 0))] * 3
        args += list(rope_tabs)
    out_specs = [row(w), pl.BlockSpec((tm // CHUNK, w, CHUNK), lambda i: (i, 0, 0)), row(w), row(n - 3 * w)]
    out_shape = [jax.ShapeDtypeStruct((m, w), BF16), jax.ShapeDtypeStruct((m // CHUNK, w, CHUNK), BF16),
                 jax.ShapeDtypeStruct((m, w), BF16), jax.ShapeDtypeStruct((m, n - 3 * w), BF16)]
    if emit_kv:
        out_specs += [row(w), row(w)]
        out_shape += [jax.ShapeDtypeStruct((m, w), F32)] * 2
    return pl.pallas_call(
        functools.partial(_in_proj0_body, rope=rope, emit_kv=emit_kv),
        grid=(m // tm,),
        in_specs=in_specs,
        out_specs=out_specs,
        out_shape=out_shape,
        compiler_params=_cparams("parallel"),
        name="in_proj_even",
    )(*args)


def _ctx_prep_body(k_ref, v_ref, kt_ref, vo_ref):
    _transposed_chunks(k_ref[0], kt_ref)
    vo_ref[0] = v_ref[0].astype(BF16)


def _ctx_prep(ctx_k, ctx_v):
    b, p, w = ctx_k.shape
    return pl.pallas_call(
        _ctx_prep_body,
        grid=(b,),
        in_specs=[pl.BlockSpec((1, p, w), lambda bi: (bi, 0, 0))] * 2,
        out_specs=[pl.BlockSpec((p // CHUNK, w, CHUNK), lambda bi: (bi, 0, 0)),
                   pl.BlockSpec((1, p, w), lambda bi: (bi, 0, 0))],
        out_shape=[jax.ShapeDtypeStruct((b * p // CHUNK, w, CHUNK), BF16), jax.ShapeDtypeStruct((b, p, w), BF16)],
        compiler_params=_cparams("parallel"),
        name="ctx_prep",
    )(ctx_k, ctx_v)


def _attn_body(*refs, has_ctx, tk, lam_init):
    if has_ctx:
        q_ref, kt_ref, v_ref, ckt_ref, cv_ref, lam_ref, sw_ref, o_ref = refs
    else:
        q_ref, kt_ref, v_ref, lam_ref, sw_ref, o_ref = refs
    q = q_ref[0]
    tq = q.shape[0]
    lane = lax.broadcasted_iota(jnp.int32, q.shape, 1)
    zero = jnp.zeros_like(q)
    qq = jnp.concatenate([jnp.where(lane < A_QK, q, zero), jnp.where(lane >= A_QK, q, zero)], axis=0)

    m = jnp.full((2 * tq, 1), NEG_BIG, F32)
    l = jnp.zeros((2 * tq, 1), F32)
    acc = jnp.zeros((2 * tq, A_V), F32)

    sources = ([(ckt_ref, cv_ref)] if has_ctx else []) + [(kt_ref, v_ref)]
    for k_src, v_src in sources:
        lk = v_src.shape[1]
        ck = _pick_tile(lk, tk)
        per = ck // CHUNK
        for c in range(lk // ck):
            kt = jnp.concatenate([k_src[c * per + r] for r in range(per)], axis=1)
            vv = v_src[0, c * ck:(c + 1) * ck, :]
            s = _dot(qq, kt)
            m_new = jnp.maximum(m, jnp.max(s, axis=-1, keepdims=True))
            alpha = jnp.exp2(m - m_new)
            p = jnp.exp2(s - m_new)
            l = alpha * l + jnp.sum(p, axis=-1, keepdims=True)
            acc = alpha * acc + _dot(p.astype(BF16), vv)
            m = m_new

    o = acc / l
    lam = (jnp.exp(jnp.sum(lam_ref[0:1, :] * lam_ref[1:2, :], axis=-1, keepdims=True))
           - jnp.exp(jnp.sum(lam_ref[2:3, :] * lam_ref[3:4, :], axis=-1, keepdims=True)) + lam_init)
    a = o[:tq] - lam * o[tq:]
    o_ref[0] = (_rms(a, sw_ref[...]) * (1.0 - lam_init)).astype(o_ref.dtype)


def _attention(q, kt, v, ctx, lam4, subln_w, lam_init, tq, tk):
    b, l, w = q.shape
    has_ctx = ctx is not None
    kt_spec = lambda n: pl.BlockSpec((n // CHUNK, A_V, CHUNK), lambda bi, h, i: (bi, h, 0))
    v_spec = lambda n: pl.BlockSpec((1, n, A_V), lambda bi, h, i: (bi, 0, h))
    in_specs = [pl.BlockSpec((1, tq, A_V), lambda bi, h, i: (bi, i, h)), kt_spec(l), v_spec(l)]
    args = [q, kt, v]
    if has_ctx:
        p = ctx[1].shape[1]
        in_specs += [kt_spec(p), v_spec(p)]
        args += list(ctx)
    in_specs += [pl.BlockSpec((4, A_QK), lambda bi, h, i: (0, 0)),
                 pl.BlockSpec((1, A_V), lambda bi, h, i: (0, 0))]
    args += [lam4, subln_w.reshape(1, A_V)]
    return pl.pallas_call(
        functools.partial(_attn_body, has_ctx=has_ctx, tk=tk, lam_init=lam_init),
        grid=(b, A_HEADS, l // tq),
        in_specs=in_specs,
        out_specs=pl.BlockSpec((1, tq, A_V), lambda bi, h, i: (bi, i, h)),
        out_shape=jax.ShapeDtypeStruct((b, l, w), BF16),
        compiler_params=_cparams("parallel", "parallel", "parallel"),
        name="diff_attention",
    )(*args)


def _out_proj0_body(a_ref, gu_ref, gv_ref, gnw_ref, ws_ref, bs_ref, wo_ref, x_ref, mod_ref, o_ref):
    tm = x_ref.shape[0]
    bw = gu_ref.shape[1]
    ch = bw // B_GROUPS
    nch = tm // CHUNK
    u = _gelu(gu_ref[...].astype(F32))
    vn = _rms(_gelu(gv_ref[...].astype(F32)), gnw_ref[...]).astype(BF16)
    parts = []
    for g in range(B_GROUPS):
        rhs = jnp.concatenate([vn[r * CHUNK:(r + 1) * CHUNK, g * ch:(g + 1) * ch] for r in range(nch)], axis=1)
        parts.append(_dot(ws_ref[g], rhs) + bs_ref[:, g:g + 1])
    s = jnp.concatenate(
        [jnp.concatenate([parts[g][:, r * ch:(r + 1) * ch] for g in range(B_GROUPS)], axis=1) for r in range(nch)],
        axis=0)
    gg = (u * s).astype(BF16)
    aw = a_ref.shape[1]
    y = _dot(a_ref[...], wo_ref[0:aw, :]) + _dot(gg, wo_ref[aw:, :])
    o_ref[...] = x_ref[...] + mod_ref[0, 2:3, :] * y


def _out_proj0(a, g, gnw, ws, bs_t, wo, x, mod, rows_per_mod, tm):
    m, d = x.shape
    aw = a.shape[1]
    bw = gnw.shape[0]
    return pl.pallas_call(
        _out_proj0_body,
        grid=(m // tm,),
        in_specs=[pl.BlockSpec((tm, aw), lambda i: (i, 0)),
                  pl.BlockSpec((tm, bw), lambda i: (i, 0)),
                  pl.BlockSpec((tm, bw), lambda i: (i, 1)),
                  pl.BlockSpec((1, bw), lambda i: (0, 0)),
                  pl.BlockSpec(ws.shape, lambda i: (0, 0, 0)),
                  pl.BlockSpec(bs_t.shape, lambda i: (0, 0)),
                  pl.BlockSpec(wo.shape, lambda i: (0, 0)),
                  pl.BlockSpec((tm, d), lambda i: (i, 0)),
                  pl.BlockSpec((1, N_MOD, d), lambda i: ((i * tm) // rows_per_mod, 0, 0))],
        out_specs=pl.BlockSpec((tm, d), lambda i: (i, 0)),
        out_shape=jax.ShapeDtypeStruct((m, d), F32),
        compiler_params=_cparams("parallel"),
        name="gmlp_out_proj",
    )(a, g, g, gnw.reshape(1, bw), ws, bs_t, wo, x, mod)


def _ffn_body(*refs, seq_len, fc, final):
    if final:
        x_ref, xp_ref, xn_ref, nw_ref, mod_ref, wa_ref, wg_ref, cw_ref, cb_ref, wd_ref, fw_ref, o_ref = refs
    else:
        x_ref, xp_ref, xn_ref, nw_ref, mod_ref, wa_ref, wg_ref, cw_ref, cb_ref, wd_ref, o_ref = refs
    tm, d = x_ref.shape
    dff = wa_ref.shape[1]
    x = x_ref[...]
    xe = jnp.concatenate([xp_ref[...], x, xn_ref[...]], axis=0)
    he = _modulated(xe, nw_ref[...], mod_ref[0, 3:4, :], mod_ref[0, 4:5, :]).astype(BF16)
    hc = he[HALO:HALO + tm]
    row0 = pl.program_id(0) * tm
    acc = jnp.zeros((tm, d), F32)
    for c in range(dff // fc):
        cs = slice(c * fc, (c + 1) * fc)
        ae = _dot(he, wa_ref[:, cs])
        g = _dot(hc, wg_ref[:, cs])
        conv = _conv3(ae, tm, seq_len, row0, cw_ref.at[:, cs], cb_ref.at[:, cs])
        acc = acc + _dot((_gelu(conv) * g).astype(BF16), wd_ref[cs, :])
    out = x + mod_ref[0, 5:6, :] * acc
    if final:
        out = _rms(out, fw_ref[...])
    o_ref[...] = out


def _halo_specs(tm, m, width, col):
    nb = tm // HALO
    last = m // HALO - 1
    return [pl.BlockSpec((HALO, width), lambda i: (jnp.maximum(i * nb - 1, 0), col)),
            pl.BlockSpec((HALO, width), lambda i: (jnp.minimum((i + 1) * nb, last), col))]


def _conv_ffn(x, nw, mod, wa, wg, cw, cb, wd, fw, rows_per_mod, seq_len, tm, fc):
    m, d = x.shape
    dff = wa.shape[1]
    final = fw is not None
    const = lambda i: (0, 0)
    in_specs = ([pl.BlockSpec((tm, d), lambda i: (i, 0))] + _halo_specs(tm, m, d, 0)
                + [pl.BlockSpec((1, d), const),
                   pl.BlockSpec((1, N_MOD, d), lambda i: ((i * tm) // rows_per_mod, 0, 0)),
                   pl.BlockSpec((d, dff), const),
                   pl.BlockSpec((d, dff), const),
                   pl.BlockSpec((3, dff), const),
                   pl.BlockSpec((1, dff), const),
                   pl.BlockSpec((dff, d), const)])
    args = [x, x, x, nw.reshape(1, d), mod, wa, wg, cw, cb.reshape(1, dff), wd]
    if final:
        in_specs.append(pl.BlockSpec((1, d), const))
        args.append(fw.reshape(1, d))
    return pl.pallas_call(
        functools.partial(_ffn_body, seq_len=seq_len, fc=fc, final=final),
        grid=(m // tm,),
        in_specs=in_specs,
        out_specs=pl.BlockSpec((tm, d), lambda i: (i, 0)),
        out_shape=jax.ShapeDtypeStruct((m, d), F32),
        compiler_params=_cparams("parallel"),
        name="conv_ffn",
    )(*args)


def _in_proj1_body(x_ref, xp_ref, xn_ref, nw_ref, mod_ref, wm_ref, wr_ref, cw_ref, cb_ref, wq_ref, wk_ref, wv_ref,
                   q_ref, kt_ref, v_ref, xc_ref, og_ref, gt_ref, *, seq_len):
    tm = x_ref.shape[0]
    ci = wm_ref.shape[1]
    hd = ci // C_HEADS
    xe = jnp.concatenate([xp_ref[...], x_ref[...], xn_ref[...]], axis=0)
    he = _modulated(xe, nw_ref[...], mod_ref[0, 0:1, :], mod_ref[0, 1:2, :]).astype(BF16)
    xme = _dot(he, wm_ref[...])
    rest = _dot(he[HALO:HALO + tm], wr_ref[...])
    og_ref[...] = rest[:, 0:ci].astype(BF16)
    gt_ref[...] = rest[:, ci:]
    conv = _conv3(xme, tm, seq_len, pl.program_id(0) * tm, cw_ref, cb_ref)
    xc = conv * _sigmoid(conv)
    xcb = xc.astype(BF16)
    xc_ref[...] = xcb
    xmb = xme[HALO:HALO + tm].astype(BF16)
    for h in range(C_HEADS):
        hs = slice(h * hd, (h + 1) * hd)
        q_ref[:, hs] = _dot(xcb[:, hs], wq_ref[h]).astype(BF16)
        k = _dot(xcb[:, hs], wk_ref[h]) * hd ** -0.5
        for r in range(tm // CHUNK):
            kt_ref[r, hs, :] = k[r * CHUNK:(r + 1) * CHUNK].T.astype(BF16)
        v_ref[:, hs] = _dot(xmb[:, hs], wv_ref[h]).astype(BF16)


def _in_proj1(x, nw, mod, wm, wr, cw, cb, wq, wk, wv, rows_per_mod, seq_len, tm):
    m, d = x.shape
    ci = wm.shape[1]
    ng = wr.shape[1] - ci
    const = lambda i: (0, 0)
    const3 = lambda i: (0, 0, 0)
    row = pl.BlockSpec((tm, ci), lambda i: (i, 0))
    return pl.pallas_call(
        functools.partial(_in_proj1_body, seq_len=seq_len),
        grid=(m // tm,),
        in_specs=[pl.BlockSpec((tm, d), lambda i: (i, 0))] + _halo_specs(tm, m, d, 0)
        + [pl.BlockSpec((1, d), const),
           pl.BlockSpec((1, N_MOD, d), lambda i: ((i * tm) // rows_per_mod, 0, 0)),
           pl.BlockSpec(wm.shape, const), pl.BlockSpec(wr.shape, const),
           pl.BlockSpec((3, ci), const), pl.BlockSpec((1, ci), const),
           pl.BlockSpec(wq.shape, const3), pl.BlockSpec(wk.shape, const3), pl.BlockSpec(wv.shape, const3)],
        out_specs=[row, pl.BlockSpec((tm // CHUNK, ci, CHUNK), lambda i: (i, 0, 0)), row, row, row,
                   pl.BlockSpec((tm, ng), lambda i: (i, 0))],
        out_shape=[jax.ShapeDtypeStruct((m, ci), BF16), jax.ShapeDtypeStruct((m // CHUNK, ci, CHUNK), BF16)]
        + [jax.ShapeDtypeStruct((m, ci), BF16)] * 3 + [jax.ShapeDtypeStruct((m, ng), F32)],
        compiler_params=_cparams("parallel"),
        name="in_proj_odd",
    )(x, x, x, nw.reshape(1, d), mod, wm, wr, cw, cb.reshape(1, ci), wq, wk, wv)


def _split3(x):
    hi = x.astype(BF16)
    r1 = x - hi.astype(F32)
    mid = r1.astype(BF16)
    lo = (r1 - mid.astype(F32)).astype(BF16)
    return hi, mid, lo


def _mlstm_scan_body(*refs, zero_init, cpb):
    it = iter(refs)
    fwd = [next(it) for _ in range(4)]
    bwd = [next(it) for _ in range(4)]
    bg_ref = next(it)
    init = None if zero_init else [next(it) for _ in range(3)]
    hf_ref, hb_ref, c_ref, n_ref, m_ref, cx_ref = [next(it) for _ in range(6)]
    t = CHUNK
    hd = c_ref.shape[-1]
    nw = cx_ref.shape[-1] - hd
    j = pl.program_id(1)

    @pl.when(j == 0)
    def _():
        for dr in range(2):
            for h in range(C_HEADS):
                if zero_init:
                    cx_ref[dr, h] = jnp.zeros(cx_ref.shape[2:], F32)
                else:
                    cx_ref[dr, h, :, 0:hd] = init[0][0, dr, h]
                    cx_ref[dr, h, :, hd:] = jnp.broadcast_to(init[1][0, dr, h], (nw, hd)).T
        m_ref[...] = jnp.zeros_like(m_ref) if zero_init else init[2][...]

    rr = lax.broadcasted_iota(jnp.int32, (t, t), 0)
    cc = lax.broadcasted_iota(jnp.int32, (t, t), 1)
    ones = jnp.ones((t, nw), BF16)

    srcs = ((fwd, hf_ref), (bwd, hb_ref))
    chunk_of = lambda ci, dr: ci if dr == 0 else cpb - 1 - ci

    m_cur = {(dr, h): m_ref[0, dr, h][:, 0:1] for dr in range(2) for h in range(C_HEADS)}
    pre = {}
    for ci in range(cpb):
        for dr in range(2):
            g_ref = srcs[dr][0][3]
            ch = chunk_of(ci, dr)
            mask = (rr >= cc) if dr == 0 else (rr <= cc)
            tri = jnp.where(mask, 1.0, 0.0).astype(BF16)
            gates = g_ref[0, ch * t:(ch + 1) * t, :] + bg_ref[...]
            hi, mid, lo = _split3(_log_sigmoid(gates))
            bsum = _dot(tri, hi) + _dot(tri, mid) + _dot(tri, lo)
            bsum_t = bsum.T
            gates_t = gates.T
            for h in range(C_HEADS):
                col_i = 2 * dr * C_HEADS + h
                col_f = (2 * dr + 1) * C_HEADS + h
                b_col = jnp.broadcast_to(bsum[:, col_f:col_f + 1], (t, t))
                b_row = bsum_t[col_f:col_f + 1, :]
                i_row = gates_t[col_i:col_i + 1, :]
                m_prev = m_cur[dr, h]
                a_inter = b_col + m_prev
                dmat = jnp.where(mask, b_col - b_row + i_row, -jnp.inf)
                m_t = jnp.maximum(a_inter, jnp.max(dmat, axis=1, keepdims=True))
                b_end = b_row[:, t - 1:t] if dr == 0 else b_row[:, 0:1]
                g_row = b_end - b_row + i_row
                m_new = jnp.maximum(b_end + m_prev, jnp.max(g_row, axis=1, keepdims=True))
                m_cur[dr, h] = m_new
                pre[ci, dr, h] = (jnp.where(mask, jnp.exp(dmat - m_t), 0.0),
                                  jnp.exp(a_inter - m_t),
                                  jnp.exp(-m_t),
                                  jnp.exp(b_end + m_prev - m_new),
                                  jnp.exp(g_row - m_new))

    for ci in range(cpb):
        for dr in range(2):
            (q_ref, kt_ref, v_ref, _), h_out = srcs[dr]
            ch = chunk_of(ci, dr)
            rows = slice(ch * t, (ch + 1) * t)
            for h in range(C_HEADS):
                decay_w, w_inter, floor, decay, w_row = pre[ci, dr, h]
                hs = slice(h * hd, (h + 1) * hd)
                cx = cx_ref[dr, h]
                qc = q_ref[0, rows, hs]
                v1 = jnp.concatenate([v_ref[0, rows, hs], ones], axis=1)
                ktc = kt_ref[ch, hs, :]
                s = _dot(qc, ktc) * decay_w
                intra = _dot(s.astype(BF16), v1)
                inter = _dot(qc, cx.astype(BF16))
                den = intra[:, hd:] + w_inter * inter[:, hd:]
                inv = 1.0 / jnp.maximum(jnp.abs(den), floor)
                reps = hd // t
                num = intra[:, 0:hd] + jnp.concatenate([w_inter] * reps, axis=1) * inter[:, 0:hd]
                h_out[0, rows, hs] = (num * jnp.concatenate([inv] * reps, axis=1)).astype(h_out.dtype)
                kw_t = (ktc.astype(F32) * w_row).astype(BF16)
                cx_ref[dr, h] = decay * cx + _dot(kw_t, v1)

    for (dr, h), m_new in m_cur.items():
        m_ref[0, dr, h] = jnp.broadcast_to(m_new, (1, m_ref.shape[-1]))

    @pl.when(j == pl.num_programs(1) - 1)
    def _():
        for dr in range(2):
            for h in range(C_HEADS):
                c_ref[0, dr, h] = cx_ref[dr, h, :, 0:hd]
                n_ref[0, dr, h] = cx_ref[dr, h, :, hd:].T[0:1, :]


def _mlstm_scan(q, kt, v, gates, bg, init, lb):
    b, l, ci = q.shape
    hd = ci // C_HEADS
    nb = l // lb
    cpb = lb // CHUNK
    zero_init = init is None
    in_specs, args = [], []
    for blk in (lambda j: j, lambda j: nb - 1 - j):
        row = pl.BlockSpec((1, lb, ci), lambda bi, j, blk=blk: (bi, blk(j), 0))
        in_specs += [row,
                     pl.BlockSpec((cpb, ci, CHUNK), lambda bi, j, blk=blk: (bi * nb + blk(j), 0, 0)),
                     row,
                     pl.BlockSpec((1, lb, gates.shape[2]), lambda bi, j, blk=blk: (bi, blk(j), 0))]
        args += [q, kt, v, gates]
    in_specs.append(pl.BlockSpec((1, 128), lambda bi, j: (0, 0)))
    args.append(bg)
    st = lambda *tail: pl.BlockSpec((1, 2, C_HEADS) + tail, lambda bi, j: (bi, 0, 0, 0, 0))
    states = [st(hd, hd), st(1, hd), st(1, 128)]
    if not zero_init:
        in_specs += states
        args += list(init)
    return pl.pallas_call(
        functools.partial(_mlstm_scan_body, zero_init=zero_init, cpb=cpb),
        grid=(b, nb),
        in_specs=in_specs,
        out_specs=[pl.BlockSpec((1, lb, ci), lambda bi, j: (bi, j, 0)),
                   pl.BlockSpec((1, lb, ci), lambda bi, j: (bi, nb - 1 - j, 0))] + states,
        out_shape=[jax.ShapeDtypeStruct((b, l, ci), BF16),
                   jax.ShapeDtypeStruct((b, l, ci), BF16),
                   jax.ShapeDtypeStruct((b, 2, C_HEADS, hd, hd), F32),
                   jax.ShapeDtypeStruct((b, 2, C_HEADS, 1, hd), F32),
                   jax.ShapeDtypeStruct((b, 2, C_HEADS, 1, 128), F32)],
        scratch_shapes=[pltpu.VMEM((2, C_HEADS, hd, hd + CHUNK), F32)],
        compiler_params=_cparams("parallel", "arbitrary"),
        name="mlstm_scan",
    )(*args)


def _out_proj1_body(hf_ref, hb_ref, og_ref, xc_ref, hn_ref, sk_ref, wo_ref, x_ref, mod_ref, o_ref):
    ci = og_ref.shape[1]
    hd = ci // C_HEADS
    hs = hf_ref[...].astype(F32) + hb_ref[...].astype(F32)
    hn = jnp.concatenate([_rms(hs[:, h * hd:(h + 1) * hd], hn_ref[:, h * hd:(h + 1) * hd]) for h in range(C_HEADS)],
                         axis=1)
    y = _sigmoid(og_ref[...].astype(F32)) * (hn + sk_ref[...] * xc_ref[...].astype(F32))
    o_ref[...] = x_ref[...] + mod_ref[0, 2:3, :] * _dot(y.astype(BF16), wo_ref[...])


def _out_proj1(hf, hb, og, xc, hnw, skip, wo, x, mod, rows_per_mod, tm):
    m, d = x.shape
    ci = xc.shape[1]
    const = lambda i: (0, 0)
    row = pl.BlockSpec((tm, ci), lambda i: (i, 0))
    return pl.pallas_call(
        _out_proj1_body,
        grid=(m // tm,),
        in_specs=[row, row, row, row,
                  pl.BlockSpec((1, ci), const),
                  pl.BlockSpec((1, ci), const),
                  pl.BlockSpec(wo.shape, const),
                  pl.BlockSpec((tm, d), lambda i: (i, 0)),
                  pl.BlockSpec((1, N_MOD, d), lambda i: ((i * tm) // rows_per_mod, 0, 0))],
        out_specs=pl.BlockSpec((tm, d), lambda i: (i, 0)),
        out_shape=jax.ShapeDtypeStruct((m, d), F32),
        compiler_params=_cparams("parallel"),
        name="mlstm_out_proj",
    )(hf, hb, og, xc, hnw.reshape(1, ci), skip.reshape(1, ci), wo, x, mod)


def _rope_tables(length):
    n_rows = length // GRID_W
    rows = jnp.repeat(jnp.arange(n_rows, dtype=F32), GRID_W)
    cols = jnp.tile(jnp.arange(GRID_W, dtype=F32), n_rows)
    inv = ROPE_BASE ** (-jnp.arange(N_FREQ, dtype=F32) / N_FREQ)
    ang = jnp.stack([rows[:, None] * inv, cols[:, None] * inv], axis=1)
    cos, sin = jnp.cos(ang), jnp.sin(ang)
    zeros = jnp.zeros_like(sin)
    lanes = lambda first, second: jnp.tile(
        jnp.stack([first, second], axis=2).reshape(length, A_QK), (1, A_V // A_QK))
    return lanes(cos, cos), lanes(-sin, zeros), lanes(zeros, sin)


def _pick_tile(n, want):
    t = min(n, want)
    while n % t:
        t //= 2
    return t


def _run_group(x3, mods, ctx, init, p, tiles):
    b, l, d = x3.shape
    m = b * l
    rows_per_mod = l if mods.shape[1] > 1 else m
    tm = _pick_tile(l if mods.shape[1] > 1 else m, tiles["tm"])
    x = x3.reshape(m, d)
    extras = {}
    depth = mods.shape[0]
    for layer in range(depth):
        mod = mods[layer]
        if layer % 2 == 0:
            e = layer // 2
            lam_init = 0.8 - 0.6 * math.exp(-0.3 * layer)
            rope = _rope_tables(l) if ctx is not None else None
            outs = _in_proj0(x, p["norm1_w"][layer], mod, p["w_in0"][e], rope, ctx is None, rows_per_mod, l, tm)
            q, kt, v, g = outs[:4]
            ctx_e = _ctx_prep(ctx[0][:, e].reshape(b, -1, A_WIDTH), ctx[1][:, e].reshape(b, -1, A_WIDTH)) \
                if ctx is not None else None
            lam4 = jnp.stack([p["lam_q1"][e], p["lam_k1"][e], p["lam_q2"][e], p["lam_k2"][e]])
            a = _attention(q.reshape(b, l, A_WIDTH), kt, v.reshape(b, l, A_WIDTH), ctx_e, lam4, p["subln_w"][e],
                           lam_init, _pick_tile(l, tiles["tq"]), tiles["tk"])
            x = _out_proj0(a.reshape(m, A_WIDTH), g, p["gate_norm_w"][e], p["w_spatial"][e],
                           p["b_spatial"][e].T, p["w_out0"][e], x, mod, rows_per_mod, tm)
            for name, kv in zip("kv", outs[4:]):
                extras.setdefault(name, []).append(kv.reshape(b, l, A_HEADS, A_V))
        else:
            o = layer // 2
            ci = p["w_out1"].shape[1]
            q, kt, v, xc, og, gates = _in_proj1(x, p["norm1_w"][layer], mod, p["w_in1_m"][o], p["w_in1_r"][o],
                                                p["mconv_w"][o], p["mconv_b"][o], p["w_q"][o], p["w_k"][o],
                                                p["w_v"][o], rows_per_mod, l, tm)
            init_o = None if init is None else tuple(s[:, o] for s in init)
            hf, hb, c_fin, n_fin, m_fin = _mlstm_scan(q.reshape(b, l, ci), kt, v.reshape(b, l, ci),
                                                      gates.reshape(b, l, -1), p["b_gates"][o], init_o,
                                                      _pick_tile(l, tiles["lb"]))
            x = _out_proj1(hf.reshape(m, ci), hb.reshape(m, ci), og, xc, p["head_norm_w"][o], p["skip_w"][o],
                           p["w_out1"][o], x, mod, rows_per_mod, tm)
            extras.setdefault("C", []).append(c_fin)
            extras.setdefault("n", []).append(n_fin[:, :, :, 0, :])
            extras.setdefault("m", []).append(m_fin[:, :, :, 0, 0])
        fw = p["final_norm_w"] if layer == depth - 1 else None
        x = _conv_ffn(x, p["norm2_w"][layer], mod, p["w_up_a"][layer], p["w_up_g"][layer], p["fconv_w"][layer],
                      p["fconv_b"][layer], p["w_down"][layer], fw, rows_per_mod, l,
                      _pick_tile(rows_per_mod, tiles["tm_ffn"]), tiles["fc"])
    return x.reshape(b, l, d), extras


def kernel(x_prompt, x_sample, cache_k, cache_v, state_C, state_n, state_m, c, c_ctx, w_mod, b_mod, norm1_w, norm2_w,
           w_in0, lam_q1, lam_k1, lam_q2, lam_k2, subln_w, gate_norm_w, w_spatial, b_spatial, w_out0, w_in1,
           b_gates, mconv_w, mconv_b, w_q, w_k, w_v, head_norm_w, skip_w, w_out1, w_up, fconv_w, fconv_b, w_down,
           final_norm_w):
    depth, d, _ = w_mod.shape
    dec_b = x_sample.shape[0]
    dff = w_down.shape[1]
    ci = w_out1.shape[1]

    n_cond = -(-(dec_b + 1) // 8) * 8
    cond = jnp.zeros((n_cond, d), F32).at[:dec_b].set(c).at[dec_b].set(c_ctx)
    mods = _modulation(cond, w_mod, b_mod).reshape(depth, n_cond, N_MOD, d)

    n_in1 = w_in1.shape[2]
    pad1 = -(-n_in1 // 128) * 128 - n_in1
    p = dict(
        norm1_w=norm1_w, norm2_w=norm2_w, lam_q1=lam_q1, lam_k1=lam_k1, lam_q2=lam_q2, lam_k2=lam_k2,
        subln_w=subln_w, gate_norm_w=gate_norm_w, b_spatial=b_spatial, mconv_w=mconv_w, mconv_b=mconv_b,
        head_norm_w=head_norm_w, skip_w=skip_w, fconv_w=fconv_w, fconv_b=fconv_b, final_norm_w=final_norm_w,
        w_in0=w_in0.astype(BF16), w_spatial=w_spatial.astype(BF16), w_out0=w_out0.astype(BF16),
        w_in1_m=w_in1[:, :, :ci].astype(BF16),
        w_in1_r=jnp.pad(w_in1[:, :, ci:], ((0, 0), (0, 0), (0, pad1))).astype(BF16),
        b_gates=jnp.pad(b_gates, ((0, 0), (0, 128 - b_gates.shape[1])))[:, None, :],
        w_q=w_q.astype(BF16), w_k=w_k.astype(BF16), w_v=w_v.astype(BF16), w_out1=w_out1.astype(BF16),
        w_up_a=w_up[:, :, :dff].astype(BF16), w_up_g=w_up[:, :, dff:].astype(BF16), w_down=w_down.astype(BF16),
    )
    tiles = dict(tm=512, tm_ffn=512, tq=1024, tk=1024, fc=2816, lb=512)

    y_prompt, ex = _run_group(x_prompt, mods[:, dec_b:dec_b + 1], None, None, p, tiles)
    init = (state_C.astype(F32),
            state_n.astype(F32)[:, :, :, :, None, :],
            jnp.broadcast_to(state_m.astype(F32)[:, :, :, :, None, None], state_m.shape + (1, 128)))
    y_sample, _ = _run_group(x_sample, mods[:, :dec_b], (cache_k, cache_v), init, p, tiles)

    return (y_prompt, y_sample, jnp.stack(ex["k"], axis=1), jnp.stack(ex["v"], axis=1),
            jnp.stack(ex["C"], axis=1), jnp.stack(ex["n"], axis=1), jnp.stack(ex["m"], axis=1))
```

```python
import functools
import math

import jax
import jax.numpy as jnp
from jax import lax
from jax.experimental import pallas as pl
from jax.experimental.pallas import tpu as pltpu

F32 = jnp.float32
BF16 = jnp.bfloat16

EPS = 1e-6
ROPE_BASE = 10000.0
GRID_W = 64
N_MOD = 6
A_HEADS = 4
A_QK = 64
A_V = 2 * A_QK
A_WIDTH = A_HEADS * A_V
N_FREQ = A_QK // 4
B_GROUPS = 4
CHUNK = 128
C_HEADS = 4
LOG2E = 1.4426950408889634

HALO = 8
VMEM_LIMIT = 56 * 1024 * 1024
NEG_BIG = -1e30


def _cparams(*sem):
    return pltpu.CompilerParams(dimension_semantics=sem, vmem_limit_bytes=VMEM_LIMIT)


def _gelu(x):
    return 0.5 * x * (1.0 + jnp.tanh(math.sqrt(2.0 / math.pi) * (x + 0.044715 * (x * x * x))))


def _sigmoid(x):
    return 1.0 / (1.0 + jnp.exp(-x))


def _log_sigmoid(x):
    return jnp.minimum(x, 0.0) - jnp.log1p(jnp.exp(-jnp.abs(x)))


def _rms(x, w):
    return x * lax.rsqrt(jnp.mean(x * x, axis=-1, keepdims=True) + EPS) * w


def _modulated(x, nw, shift, scale):
    return _rms(x, nw) * (1.0 + scale) + shift


def _dot(a, b):
    return jnp.dot(a, b, preferred_element_type=F32)


def _conv3(xe, tm, seq_len, row0, w_ref, b_ref):
    n = xe.shape[0]
    prev = pltpu.roll(xe, 1, 0)[HALO:HALO + tm]
    nxt = pltpu.roll(xe, n - 1, 0)[HALO:HALO + tm]
    cur = xe[HALO:HALO + tm]
    pos = (row0 + lax.broadcasted_iota(jnp.int32, (tm, 1), 0)) % seq_len
    prev = jnp.where(pos == 0, 0.0, prev)
    nxt = jnp.where(pos == seq_len - 1, 0.0, nxt)
    return prev * w_ref[0:1, :] + cur * w_ref[1:2, :] + nxt * w_ref[2:3, :] + b_ref[...]


def _mod_body(c_ref, w_ref, b_ref, o_ref):
    c = c_ref[...]
    s = (c * _sigmoid(c)).astype(BF16)
    o_ref[0] = _dot(s, w_ref[0].astype(BF16)) + b_ref[0]


def _modulation(cond, w_mod, b_mod):
    depth, d, n = w_mod.shape
    r = cond.shape[0]
    tn = n // 4
    return pl.pallas_call(
        _mod_body,
        grid=(depth, n // tn),
        in_specs=[pl.BlockSpec((r, d), lambda l, j: (0, 0)),
                  pl.BlockSpec((1, d, tn), lambda l, j: (l, 0, j)),
                  pl.BlockSpec((1, 1, tn), lambda l, j: (l, 0, j))],
        out_specs=pl.BlockSpec((1, r, tn), lambda l, j: (l, 0, j)),
        out_shape=jax.ShapeDtypeStruct((depth, r, n), F32),
        compiler_params=_cparams("parallel", "parallel"),
        name="modulation",
    )(cond, w_mod, b_mod.reshape(depth, 1, n))


def _transposed_chunks(x, o_ref):
    for r in range(x.shape[0] // CHUNK):
        o_ref[r] = x[r * CHUNK:(r + 1) * CHUNK].T.astype(o_ref.dtype)


def _in_proj0_body(*refs, rope, emit_kv):
    it = iter(refs)
    x_ref, nw_ref, mod_ref, w_ref = [next(it) for _ in range(4)]
    tabs = [next(it) for _ in range(3)] if rope else None
    qt_ref, k_ref, vt_ref, g_ref = [next(it) for _ in range(4)]
    h = _modulated(x_ref[...], nw_ref[...], mod_ref[0, 0:1, :], mod_ref[0, 1:2, :])
    z = _dot(h.astype(BF16), w_ref[...])
    w = k_ref.shape[1]
    q, k, v = z[:, 0:w], z[:, w:2 * w], z[:, 2 * w:3 * w]
    if emit_kv:
        next(it)[...] = k
        next(it)[...] = v
    if rope:
        reps = w // tabs[0].shape[1]
        cos, sa, sb = [jnp.concatenate([tab[...]] * reps, axis=1) for tab in tabs]

        def rot(x):
            return x * cos + pltpu.roll(x, w - N_FREQ, 1) * sa + pltpu.roll(x, N_FREQ, 1) * sb

        q, k = rot(q), rot(k)
    _transposed_chunks(q * (A_QK ** -0.5 * LOG2E), qt_ref)
    k_ref[...] = k.astype(BF16)
    _transposed_chunks(v, vt_ref)
    g_ref[...] = z[:, 3 * w:].astype(g_ref.dtype)


def _in_proj0(x, nw, mod, w_in, rope_tabs, emit_kv, rows_per_mod, seq_len, tm):
    m, d = x.shape
    n = w_in.shape[1]
    w = A_WIDTH
    rope = rope_tabs is not None
    row = lambda width: pl.BlockSpec((tm, width), lambda i: (i, 0))
    in_specs = [row(d),
                pl.BlockSpec((1, d), lambda i: (0, 0)),
                pl.BlockSpec((1, N_MOD, d), lambda i: ((i * tm) // rows_per_mod, 0, 0)),
                pl.BlockSpec((d, n), lambda i: (0, 0))]
    args = [x, nw.reshape(1, d), mod, w_in]
    if rope:
        blocks = seq_len // tm
        in_specs += [pl.BlockSpec((tm, rope_tabs[0].shape[1]), lambda i: (i % blocks, 0))] * 3
        args += list(rope_tabs)
    chunks = pl.BlockSpec((tm // CHUNK, w, CHUNK), lambda i: (i, 0, 0))
    out_specs = [chunks, row(w), chunks, row(n - 3 * w)]
    out_shape = [jax.ShapeDtypeStruct((m // CHUNK, w, CHUNK), BF16), jax.ShapeDtypeStruct((m, w), BF16),
                 jax.ShapeDtypeStruct((m // CHUNK, w, CHUNK), BF16), jax.ShapeDtypeStruct((m, n - 3 * w), BF16)]
    if emit_kv:
        out_specs += [row(w), row(w)]
        out_shape += [jax.ShapeDtypeStruct((m, w), F32)] * 2
    return pl.pallas_call(
        functools.partial(_in_proj0_body, rope=rope, emit_kv=emit_kv),
        grid=(m // tm,),
        in_specs=in_specs,
        out_specs=out_specs,
        out_shape=out_shape,
        compiler_params=_cparams("parallel"),
        name="in_proj_even",
    )(*args)


def _ctx_prep_body(k_ref, v_ref, ko_ref, vt_ref):
    ko_ref[0] = k_ref[0].astype(BF16)
    _transposed_chunks(v_ref[0], vt_ref)


def _ctx_prep(ctx_k, ctx_v):
    b, p, w = ctx_k.shape
    return pl.pallas_call(
        _ctx_prep_body,
        grid=(b,),
        in_specs=[pl.BlockSpec((1, p, w), lambda bi: (bi, 0, 0))] * 2,
        out_specs=[pl.BlockSpec((1, p, w), lambda bi: (bi, 0, 0)),
                   pl.BlockSpec((p // CHUNK, w, CHUNK), lambda bi: (bi, 0, 0))],
        out_shape=[jax.ShapeDtypeStruct((b, p, w), BF16), jax.ShapeDtypeStruct((b * p // CHUNK, w, CHUNK), BF16)],
        compiler_params=_cparams("parallel"),
        name="ctx_prep",
    )(ctx_k, ctx_v)


def _joined(ref, first, count):
    return jnp.concatenate([ref[first + r] for r in range(count)], axis=1)


def _attn_body(*refs, has_ctx, tk, lam_init):
    if has_ctx:
        qt_ref, k_ref, vt_ref, ck_ref, cvt_ref, lam_ref, sw_ref, o_ref = refs
    else:
        qt_ref, k_ref, vt_ref, lam_ref, sw_ref, o_ref = refs
    qt = _joined(qt_ref, 0, qt_ref.shape[0])
    tq = qt.shape[1]
    sub = lax.broadcasted_iota(jnp.int32, qt.shape, 0)
    zero = jnp.zeros_like(qt)
    qq = jnp.concatenate([jnp.where(sub < A_QK, qt, zero), jnp.where(sub >= A_QK, qt, zero)], axis=1)

    m = jnp.full((1, 2 * tq), NEG_BIG, F32)
    l = jnp.zeros((1, 2 * tq), F32)
    acc = jnp.zeros((A_V, 2 * tq), F32)

    sources = ([(ck_ref, cvt_ref)] if has_ctx else []) + [(k_ref, vt_ref)]
    for k_src, vt_src in sources:
        lk = k_src.shape[1]
        ck = _pick_tile(lk, tk)
        per = ck // CHUNK
        for c in range(lk // ck):
            s = _dot(k_src[0, c * ck:(c + 1) * ck, :], qq)
            m_new = jnp.maximum(m, jnp.max(s, axis=0, keepdims=True))
            alpha = jnp.exp2(m - m_new)
            p = jnp.exp2(s - m_new)
            l = alpha * l + jnp.sum(p, axis=0, keepdims=True)
            acc = alpha * acc + _dot(_joined(vt_src, c * per, per), p.astype(BF16))
            m = m_new

    o = acc / l
    lam = (jnp.exp(jnp.sum(lam_ref[0:1, :] * lam_ref[1:2, :], axis=-1, keepdims=True))
           - jnp.exp(jnp.sum(lam_ref[2:3, :] * lam_ref[3:4, :], axis=-1, keepdims=True)) + lam_init)
    a = o[:, 0:tq] - lam * o[:, tq:]
    an = a * lax.rsqrt(jnp.mean(a * a, axis=0, keepdims=True) + EPS) * sw_ref[...] * (1.0 - lam_init)
    o_ref[0] = an.T.astype(o_ref.dtype)


def _attention(qt, k, vt, ctx, lam4, subln_w, lam_init, tq, tk):
    b, l, w = k.shape
    has_ctx = ctx is not None
    k_spec = lambda n: pl.BlockSpec((1, n, A_V), lambda bi, h, i: (bi, 0, h))
    vt_spec = lambda n: pl.BlockSpec((n // CHUNK, A_V, CHUNK), lambda bi, h, i: (bi, h, 0))
    nq = l // tq
    in_specs = [pl.BlockSpec((tq // CHUNK, A_V, CHUNK), lambda bi, h, i: (bi * nq + i, h, 0)), k_spec(l), vt_spec(l)]
    args = [qt, k, vt]
    if has_ctx:
        p = ctx[0].shape[1]
        in_specs += [k_spec(p), vt_spec(p)]
        args += list(ctx)
    in_specs += [pl.BlockSpec((4, A_QK), lambda bi, h, i: (0, 0)),
                 pl.BlockSpec((A_V, 1), lambda bi, h, i: (0, 0))]
    args += [lam4, subln_w.reshape(A_V, 1)]
    return pl.pallas_call(
        functools.partial(_attn_body, has_ctx=has_ctx, tk=tk, lam_init=lam_init),
        grid=(b, A_HEADS, l // tq),
        in_specs=in_specs,
        out_specs=pl.BlockSpec((1, tq, A_V), lambda bi, h, i: (bi, i, h)),
        out_shape=jax.ShapeDtypeStruct((b, l, w), BF16),
        compiler_params=_cparams("parallel", "parallel", "parallel"),
        name="diff_attention",
    )(*args)


def _out_proj0_body(a_ref, gu_ref, gv_ref, gnw_ref, ws_ref, bs_ref, wo_ref, x_ref, mod_ref, o_ref):
    tm = x_ref.shape[0]
    bw = gu_ref.shape[1]
    ch = bw // B_GROUPS
    nch = tm // CHUNK
    u = _gelu(gu_ref[...].astype(F32))
    vn = _rms(_gelu(gv_ref[...].astype(F32)), gnw_ref[...]).astype(BF16)
    parts = []
    for g in range(B_GROUPS):
        rhs = jnp.concatenate([vn[r * CHUNK:(r + 1) * CHUNK, g * ch:(g + 1) * ch] for r in range(nch)], axis=1)
        parts.append(_dot(ws_ref[g], rhs) + bs_ref[:, g:g + 1])
    s = jnp.concatenate(
        [jnp.concatenate([parts[g][:, r * ch:(r + 1) * ch] for g in range(B_GROUPS)], axis=1) for r in range(nch)],
        axis=0)
    gg = (u * s).astype(BF16)
    aw = a_ref.shape[1]
    y = _dot(a_ref[...], wo_ref[0:aw, :]) + _dot(gg, wo_ref[aw:, :])
    o_ref[...] = x_ref[...] + mod_ref[0, 2:3, :] * y


def _out_proj0(a, g, gnw, ws, bs_t, wo, x, mod, rows_per_mod, tm):
    m, d = x.shape
    aw = a.shape[1]
    bw = gnw.shape[0]
    return pl.pallas_call(
        _out_proj0_body,
        grid=(m // tm,),
        in_specs=[pl.BlockSpec((tm, aw), lambda i: (i, 0)),
                  pl.BlockSpec((tm, bw), lambda i: (i, 0)),
                  pl.BlockSpec((tm, bw), lambda i: (i, 1)),
                  pl.BlockSpec((1, bw), lambda i: (0, 0)),
                  pl.BlockSpec(ws.shape, lambda i: (0, 0, 0)),
                  pl.BlockSpec(bs_t.shape, lambda i: (0, 0)),
                  pl.BlockSpec(wo.shape, lambda i: (0, 0)),
                  pl.BlockSpec((tm, d), lambda i: (i, 0)),
                  pl.BlockSpec((1, N_MOD, d), lambda i: ((i * tm) // rows_per_mod, 0, 0))],
        out_specs=pl.BlockSpec((tm, d), lambda i: (i, 0)),
        out_shape=jax.ShapeDtypeStruct((m, d), F32),
        compiler_params=_cparams("parallel"),
        name="gmlp_out_proj",
    )(a, g, g, gnw.reshape(1, bw), ws, bs_t, wo, x, mod)


def _mlstm_mix(hf, hb, og, xc, hn_ref, sk_ref, wo_ref):
    hd = hn_ref.shape[1] // C_HEADS
    hs = hf + hb
    hn = jnp.concatenate([_rms(hs[:, h * hd:(h + 1) * hd], hn_ref[:, h * hd:(h + 1) * hd]) for h in range(C_HEADS)],
                         axis=1)
    return _dot((_sigmoid(og) * (hn + sk_ref[...] * xc)).astype(BF16), wo_ref[...])


def _ffn_body(*refs, seq_len, fc, final, premix):
    it = iter(refs)
    x_ref, xp_ref, xn_ref, nw_ref, mod_ref, wa_ref, wg_ref, cw_ref, cb_ref, wd_ref = [next(it) for _ in range(10)]
    fw_ref = next(it) if final else None
    pre = [next(it) for _ in range(15)] if premix else None
    o_ref = next(it)
    tm, d = x_ref.shape
    dff = wa_ref.shape[1]
    xe = jnp.concatenate([xp_ref[...], x_ref[...], xn_ref[...]], axis=0)
    if premix:
        def ext(k):
            main, prev, nxt = pre[3 * k:3 * k + 3]
            return jnp.concatenate([prev[...].astype(F32)[-HALO:], main[...].astype(F32), nxt[...].astype(F32)[:HALO]],
                                   axis=0)

        xe = xe + mod_ref[0, 2:3, :] * _mlstm_mix(ext(0), ext(1), ext(2), ext(3), *pre[12:15])
    x = xe[HALO:HALO + tm]
    he = _modulated(xe, nw_ref[...], mod_ref[0, 3:4, :], mod_ref[0, 4:5, :]).astype(BF16)
    hc = he[HALO:HALO + tm]
    row0 = pl.program_id(0) * tm
    acc = jnp.zeros((tm, d), F32)
    for c in range(dff // fc):
        cs = slice(c * fc, (c + 1) * fc)
        ae = _dot(he, wa_ref[:, cs])
        g = _dot(hc, wg_ref[:, cs])
        conv = _conv3(ae, tm, seq_len, row0, cw_ref.at[:, cs], cb_ref.at[:, cs])
        acc = acc + _dot((_gelu(conv) * g).astype(BF16), wd_ref[cs, :])
    out = x + mod_ref[0, 5:6, :] * acc
    if final:
        out = _rms(out, fw_ref[...])
    o_ref[...] = out


def _halo_specs(tm, m, width, rows=HALO):
    nb = tm // rows
    last = m // rows - 1
    return [pl.BlockSpec((rows, width), lambda i: (jnp.maximum(i * nb - 1, 0), 0)),
            pl.BlockSpec((rows, width), lambda i: (jnp.minimum((i + 1) * nb, last), 0))]


def _conv_ffn(x, nw, mod, wa, wg, cw, cb, wd, fw, premix, rows_per_mod, seq_len, tm, fc):
    m, d = x.shape
    dff = wa.shape[1]
    final = fw is not None
    const = lambda i: (0, 0)
    in_specs = ([pl.BlockSpec((tm, d), lambda i: (i, 0))] + _halo_specs(tm, m, d)
                + [pl.BlockSpec((1, d), const),
                   pl.BlockSpec((1, N_MOD, d), lambda i: ((i * tm) // rows_per_mod, 0, 0)),
                   pl.BlockSpec((d, dff), const),
                   pl.BlockSpec((d, dff), const),
                   pl.BlockSpec((3, dff), const),
                   pl.BlockSpec((1, dff), const),
                   pl.BlockSpec((dff, d), const)])
    args = [x, x, x, nw.reshape(1, d), mod, wa, wg, cw, cb.reshape(1, dff), wd]
    if final:
        in_specs.append(pl.BlockSpec((1, d), const))
        args.append(fw.reshape(1, d))
    if premix is not None:
        *acts, hnw, skip, wo = premix
        ci = wo.shape[0]
        for act in acts:
            in_specs += [pl.BlockSpec((tm, ci), lambda i: (i, 0))] + _halo_specs(tm, m, ci, 2 * HALO)
            args += [act, act, act]
        in_specs += [pl.BlockSpec((1, ci), const), pl.BlockSpec((1, ci), const), pl.BlockSpec(wo.shape, const)]
        args += [hnw.reshape(1, ci), skip.reshape(1, ci), wo]
    return pl.pallas_call(
        functools.partial(_ffn_body, seq_len=seq_len, fc=fc, final=final, premix=premix is not None),
        grid=(m // tm,),
        in_specs=in_specs,
        out_specs=pl.BlockSpec((tm, d), lambda i: (i, 0)),
        out_shape=jax.ShapeDtypeStruct((m, d), F32),
        compiler_params=_cparams("parallel"),
        name="conv_ffn",
    )(*args)


def _in_proj1_body(x_ref, xp_ref, xn_ref, nw_ref, mod_ref, wm_ref, wr_ref, cw_ref, cb_ref, wq_ref, wk_ref, wv_ref,
                   q_ref, kt_ref, v_ref, xc_ref, og_ref, gt_ref, *, seq_len):
    tm = x_ref.shape[0]
    ci = wm_ref.shape[1]
    hd = ci // C_HEADS
    xe = jnp.concatenate([xp_ref[...], x_ref[...], xn_ref[...]], axis=0)
    he = _modulated(xe, nw_ref[...], mod_ref[0, 0:1, :], mod_ref[0, 1:2, :]).astype(BF16)
    xme = _dot(he, wm_ref[...])
    rest = _dot(he[HALO:HALO + tm], wr_ref[...])
    og_ref[...] = rest[:, 0:ci].astype(BF16)
    gt_ref[...] = rest[:, ci:]
    conv = _conv3(xme, tm, seq_len, pl.program_id(0) * tm, cw_ref, cb_ref)
    xc = conv * _sigmoid(conv)
    xcb = xc.astype(BF16)
    xc_ref[...] = xcb
    xmb = xme[HALO:HALO + tm].astype(BF16)
    for h in range(C_HEADS):
        hs = slice(h * hd, (h + 1) * hd)
        q_ref[:, hs] = _dot(xcb[:, hs], wq_ref[h]).astype(BF16)
        k = _dot(xcb[:, hs], wk_ref[h]) * hd ** -0.5
        for r in range(tm // CHUNK):
            kt_ref[r, hs, :] = k[r * CHUNK:(r + 1) * CHUNK].T.astype(BF16)
        v_ref[:, hs] = _dot(xmb[:, hs], wv_ref[h]).astype(BF16)


def _in_proj1(x, nw, mod, wm, wr, cw, cb, wq, wk, wv, rows_per_mod, seq_len, tm):
    m, d = x.shape
    ci = wm.shape[1]
    ng = wr.shape[1] - ci
    const = lambda i: (0, 0)
    const3 = lambda i: (0, 0, 0)
    row = pl.BlockSpec((tm, ci), lambda i: (i, 0))
    return pl.pallas_call(
        functools.partial(_in_proj1_body, seq_len=seq_len),
        grid=(m // tm,),
        in_specs=[pl.BlockSpec((tm, d), lambda i: (i, 0))] + _halo_specs(tm, m, d)
        + [pl.BlockSpec((1, d), const),
           pl.BlockSpec((1, N_MOD, d), lambda i: ((i * tm) // rows_per_mod, 0, 0)),
           pl.BlockSpec(wm.shape, const), pl.BlockSpec(wr.shape, const),
           pl.BlockSpec((3, ci), const), pl.BlockSpec((1, ci), const),
           pl.BlockSpec(wq.shape, const3), pl.BlockSpec(wk.shape, const3), pl.BlockSpec(wv.shape, const3)],
        out_specs=[row, pl.BlockSpec((tm // CHUNK, ci, CHUNK), lambda i: (i, 0, 0)), row, row, row,
                   pl.BlockSpec((tm, ng), lambda i: (i, 0))],
        out_shape=[jax.ShapeDtypeStruct((m, ci), BF16), jax.ShapeDtypeStruct((m // CHUNK, ci, CHUNK), BF16)]
        + [jax.ShapeDtypeStruct((m, ci), BF16)] * 3 + [jax.ShapeDtypeStruct((m, ng), F32)],
        compiler_params=_cparams("parallel"),
        name="in_proj_odd",
    )(x, x, x, nw.reshape(1, d), mod, wm, wr, cw, cb.reshape(1, ci), wq, wk, wv)


def _split3(x):
    hi = x.astype(BF16)
    r1 = x - hi.astype(F32)
    mid = r1.astype(BF16)
    lo = (r1 - mid.astype(F32)).astype(BF16)
    return hi, mid, lo


def _mlstm_scan_body(*refs, zero_init, cpb):
    it = iter(refs)
    fwd = [next(it) for _ in range(4)]
    bwd = [next(it) for _ in range(4)]
    bg_ref = next(it)
    init = None if zero_init else [next(it) for _ in range(3)]
    hf_ref, hb_ref, c_ref, n_ref, m_ref, cx_ref = [next(it) for _ in range(6)]
    t = CHUNK
    hd = c_ref.shape[-1]
    nw = cx_ref.shape[-1] - hd
    j = pl.program_id(1)

    @pl.when(j == 0)
    def _():
        for dr in range(2):
            for h in range(C_HEADS):
                if zero_init:
                    cx_ref[dr, h] = jnp.zeros(cx_ref.shape[2:], F32)
                else:
                    cx_ref[dr, h, :, 0:hd] = init[0][0, dr, h]
                    cx_ref[dr, h, :, hd:] = jnp.broadcast_to(init[1][0, dr, h], (nw, hd)).T
        m_ref[...] = jnp.zeros_like(m_ref) if zero_init else init[2][...]

    rr = lax.broadcasted_iota(jnp.int32, (t, t), 0)
    cc = lax.broadcasted_iota(jnp.int32, (t, t), 1)
    ones = jnp.ones((t, nw), BF16)

    srcs = ((fwd, hf_ref), (bwd, hb_ref))
    chunk_of = lambda ci, dr: ci if dr == 0 else cpb - 1 - ci

    m_cur = {(dr, h): m_ref[0, dr, h][:, 0:1] for dr in range(2) for h in range(C_HEADS)}
    pre = {}
    for ci in range(cpb):
        for dr in range(2):
            g_ref = srcs[dr][0][3]
            ch = chunk_of(ci, dr)
            mask = (rr >= cc) if dr == 0 else (rr <= cc)
            tri = jnp.where(mask, 1.0, 0.0).astype(BF16)
            gates = g_ref[0, ch * t:(ch + 1) * t, :] + bg_ref[...]
            hi, mid, lo = _split3(_log_sigmoid(gates))
            bsum = _dot(tri, hi) + _dot(tri, mid) + _dot(tri, lo)
            bsum_t = bsum.T
            gates_t = gates.T
            for h in range(C_HEADS):
                col_i = 2 * dr * C_HEADS + h
                col_f = (2 * dr + 1) * C_HEADS + h
                b_col = jnp.broadcast_to(bsum[:, col_f:col_f + 1], (t, t))
                b_row = bsum_t[col_f:col_f + 1, :]
                i_row = gates_t[col_i:col_i + 1, :]
                m_prev = m_cur[dr, h]
                a_inter = b_col + m_prev
                dmat = jnp.where(mask, b_col - b_row + i_row, -jnp.inf)
                m_t = jnp.maximum(a_inter, jnp.max(dmat, axis=1, keepdims=True))
                b_end = b_row[:, t - 1:t] if dr == 0 else b_row[:, 0:1]
                g_row = b_end - b_row + i_row
                m_new = jnp.maximum(b_end + m_prev, jnp.max(g_row, axis=1, keepdims=True))
                m_cur[dr, h] = m_new
                pre[ci, dr, h] = (jnp.where(mask, jnp.exp(dmat - m_t), 0.0),
                                  jnp.exp(a_inter - m_t),
                                  jnp.exp(-m_t),
                                  jnp.exp(b_end + m_prev - m_new),
                                  jnp.exp(g_row - m_new))

    for ci in range(cpb):
        for dr in range(2):
            (q_ref, kt_ref, v_ref, _), h_out = srcs[dr]
            ch = chunk_of(ci, dr)
            rows = slice(ch * t, (ch + 1) * t)
            for h in range(C_HEADS):
                decay_w, w_inter, floor, decay, w_row = pre[ci, dr, h]
                hs = slice(h * hd, (h + 1) * hd)
                cx = cx_ref[dr, h]
                qc = q_ref[0, rows, hs]
                v1 = jnp.concatenate([v_ref[0, rows, hs], ones], axis=1)
                ktc = kt_ref[ch, hs, :]
                s = _dot(qc, ktc) * decay_w
                intra = _dot(s.astype(BF16), v1)
                inter = _dot(qc, cx.astype(BF16))
                den = intra[:, hd:] + w_inter * inter[:, hd:]
                inv = 1.0 / jnp.maximum(jnp.abs(den), floor)
                reps = hd // t
                num = intra[:, 0:hd] + jnp.concatenate([w_inter] * reps, axis=1) * inter[:, 0:hd]
                h_out[0, rows, hs] = (num * jnp.concatenate([inv] * reps, axis=1)).astype(h_out.dtype)
                kw_t = (ktc.astype(F32) * w_row).astype(BF16)
                cx_ref[dr, h] = decay * cx + _dot(kw_t, v1)

    for (dr, h), m_new in m_cur.items():
        m_ref[0, dr, h] = jnp.broadcast_to(m_new, (1, m_ref.shape[-1]))

    @pl.when(j == pl.num_programs(1) - 1)
    def _():
        for dr in range(2):
            for h in range(C_HEADS):
                c_ref[0, dr, h] = cx_ref[dr, h, :, 0:hd]
                n_ref[0, dr, h] = cx_ref[dr, h, :, hd:].T[0:1, :]


def _mlstm_scan(q, kt, v, gates, bg, init, lb):
    b, l, ci = q.shape
    hd = ci // C_HEADS
    nb = l // lb
    cpb = lb // CHUNK
    zero_init = init is None
    in_specs, args = [], []
    for blk in (lambda j: j, lambda j: nb - 1 - j):
        row = pl.BlockSpec((1, lb, ci), lambda bi, j, blk=blk: (bi, blk(j), 0))
        in_specs += [row,
                     pl.BlockSpec((cpb, ci, CHUNK), lambda bi, j, blk=blk: (bi * nb + blk(j), 0, 0)),
                     row,
                     pl.BlockSpec((1, lb, gates.shape[2]), lambda bi, j, blk=blk: (bi, blk(j), 0))]
        args += [q, kt, v, gates]
    in_specs.append(pl.BlockSpec((1, 128), lambda bi, j: (0, 0)))
    args.append(bg)
    st = lambda *tail: pl.BlockSpec((1, 2, C_HEADS) + tail, lambda bi, j: (bi, 0, 0, 0, 0))
    states = [st(hd, hd), st(1, hd), st(1, 128)]
    if not zero_init:
        in_specs += states
        args += list(init)
    return pl.pallas_call(
        functools.partial(_mlstm_scan_body, zero_init=zero_init, cpb=cpb),
        grid=(b, nb),
        in_specs=in_specs,
        out_specs=[pl.BlockSpec((1, lb, ci), lambda bi, j: (bi, j, 0)),
                   pl.BlockSpec((1, lb, ci), lambda bi, j: (bi, nb - 1 - j, 0))] + states,
        out_shape=[jax.ShapeDtypeStruct((b, l, ci), BF16),
                   jax.ShapeDtypeStruct((b, l, ci), BF16),
                   jax.ShapeDtypeStruct((b, 2, C_HEADS, hd, hd), F32),
                   jax.ShapeDtypeStruct((b, 2, C_HEADS, 1, hd), F32),
                   jax.ShapeDtypeStruct((b, 2, C_HEADS, 1, 128), F32)],
        scratch_shapes=[pltpu.VMEM((2, C_HEADS, hd, hd + CHUNK), F32)],
        compiler_params=_cparams("parallel", "arbitrary"),
        name="mlstm_scan",
    )(*args)


def _rope_tables(length):
    n_rows = length // GRID_W
    rows = jnp.repeat(jnp.arange(n_rows, dtype=F32), GRID_W)
    cols = jnp.tile(jnp.arange(GRID_W, dtype=F32), n_rows)
    inv = ROPE_BASE ** (-jnp.arange(N_FREQ, dtype=F32) / N_FREQ)
    ang = jnp.stack([rows[:, None] * inv, cols[:, None] * inv], axis=1)
    cos, sin = jnp.cos(ang), jnp.sin(ang)
    zeros = jnp.zeros_like(sin)
    lanes = lambda first, second: jnp.tile(
        jnp.stack([first, second], axis=2).reshape(length, A_QK), (1, A_V // A_QK))
    return lanes(cos, cos), lanes(-sin, zeros), lanes(zeros, sin)


def _pick_tile(n, want):
    t = min(n, want)
    while n % t:
        t //= 2
    return t


def _run_group(x3, mods, ctx, init, p, tiles):
    b, l, d = x3.shape
    m = b * l
    rows_per_mod = l if mods.shape[1] > 1 else m
    tm = _pick_tile(l if mods.shape[1] > 1 else m, tiles["tm"])
    x = x3.reshape(m, d)
    extras = {}
    depth = mods.shape[0]
    for layer in range(depth):
        mod = mods[layer]
        if layer % 2 == 0:
            e = layer // 2
            lam_init = 0.8 - 0.6 * math.exp(-0.3 * layer)
            premix = None
            rope = _rope_tables(l) if ctx is not None else None
            outs = _in_proj0(x, p["norm1_w"][layer], mod, p["w_in0"][e], rope, ctx is None, rows_per_mod, l, tm)
            qt, k, vt, g = outs[:4]
            ctx_e = _ctx_prep(ctx[0][:, e].reshape(b, -1, A_WIDTH), ctx[1][:, e].reshape(b, -1, A_WIDTH)) \
                if ctx is not None else None
            lam4 = jnp.stack([p["lam_q1"][e], p["lam_k1"][e], p["lam_q2"][e], p["lam_k2"][e]])
            a = _attention(qt, k.reshape(b, l, A_WIDTH), vt, ctx_e, lam4, p["subln_w"][e],
                           lam_init, _pick_tile(l, tiles["tq"]), tiles["tk"])
            x = _out_proj0(a.reshape(m, A_WIDTH), g, p["gate_norm_w"][e], p["w_spatial"][e],
                           p["b_spatial"][e].T, p["w_out0"][e], x, mod, rows_per_mod, tm)
            for name, kv in zip("kv", outs[4:]):
                extras.setdefault(name, []).append(kv.reshape(b, l, A_HEADS, A_V))
        else:
            o = layer // 2
            ci = p["w_out1"].shape[1]
            q, kt, v, xc, og, gates = _in_proj1(x, p["norm1_w"][layer], mod, p["w_in1_m"][o], p["w_in1_r"][o],
                                                p["mconv_w"][o], p["mconv_b"][o], p["w_q"][o], p["w_k"][o],
                                                p["w_v"][o], rows_per_mod, l, tm)
            init_o = None if init is None else tuple(s[:, o] for s in init)
            hf, hb, c_fin, n_fin, m_fin = _mlstm_scan(q.reshape(b, l, ci), kt, v.reshape(b, l, ci),
                                                      gates.reshape(b, l, -1), p["b_gates"][o], init_o,
                                                      _pick_tile(l, tiles["lb"]))
            premix = (hf.reshape(m, ci), hb.reshape(m, ci), og, xc, p["head_norm_w"][o], p["skip_w"][o], p["w_out1"][o])
            extras.setdefault("C", []).append(c_fin)
            extras.setdefault("n", []).append(n_fin[:, :, :, 0, :])
            extras.setdefault("m", []).append(m_fin[:, :, :, 0, 0])
        fw = p["final_norm_w"] if layer == depth - 1 else None
        x = _conv_ffn(x, p["norm2_w"][layer], mod, p["w_up_a"][layer], p["w_up_g"][layer], p["fconv_w"][layer],
                      p["fconv_b"][layer], p["w_down"][layer], fw, premix, rows_per_mod, l, tm, tiles["fc"])
    return x.reshape(b, l, d), extras


def kernel(x_prompt, x_sample, cache_k, cache_v, state_C, state_n, state_m, c, c_ctx, w_mod, b_mod, norm1_w, norm2_w,
           w_in0, lam_q1, lam_k1, lam_q2, lam_k2, subln_w, gate_norm_w, w_spatial, b_spatial, w_out0, w_in1,
           b_gates, mconv_w, mconv_b, w_q, w_k, w_v, head_norm_w, skip_w, w_out1, w_up, fconv_w, fconv_b, w_down,
           final_norm_w):
    depth, d, _ = w_mod.shape
    dec_b = x_sample.shape[0]
    dff = w_down.shape[1]
    ci = w_out1.shape[1]

    n_cond = -(-(dec_b + 1) // 8) * 8
    cond = jnp.zeros((n_cond, d), F32).at[:dec_b].set(c).at[dec_b].set(c_ctx)
    mods = _modulation(cond, w_mod, b_mod).reshape(depth, n_cond, N_MOD, d)

    n_in1 = w_in1.shape[2]
    pad1 = -(-n_in1 // 128) * 128 - n_in1
    p = dict(
        norm1_w=norm1_w, norm2_w=norm2_w, lam_q1=lam_q1, lam_k1=lam_k1, lam_q2=lam_q2, lam_k2=lam_k2,
        subln_w=subln_w, gate_norm_w=gate_norm_w, b_spatial=b_spatial, mconv_w=mconv_w, mconv_b=mconv_b,
        head_norm_w=head_norm_w, skip_w=skip_w, fconv_w=fconv_w, fconv_b=fconv_b, final_norm_w=final_norm_w,
        w_in0=w_in0.astype(BF16), w_spatial=w_spatial.astype(BF16), w_out0=w_out0.astype(BF16),
        w_in1_m=w_in1[:, :, :ci].astype(BF16),
        w_in1_r=jnp.pad(w_in1[:, :, ci:], ((0, 0), (0, 0), (0, pad1))).astype(BF16),
        b_gates=jnp.pad(b_gates, ((0, 0), (0, 128 - b_gates.shape[1])))[:, None, :],
        w_q=w_q.astype(BF16), w_k=w_k.astype(BF16), w_v=w_v.astype(BF16), w_out1=w_out1.astype(BF16),
        w_up_a=w_up[:, :, :dff].astype(BF16), w_up_g=w_up[:, :, dff:].astype(BF16), w_down=w_down.astype(BF16),
    )
    tiles = dict(tm=512, tq=1024, tk=512, fc=2816, lb=512)

    y_prompt, ex = _run_group(x_prompt, mods[:, dec_b:dec_b + 1], None, None, p, tiles)
    init = (state_C.astype(F32),
            state_n.astype(F32)[:, :, :, :, None, :],
            jnp.broadcast_to(state_m.astype(F32)[:, :, :, :, None, None], state_m.shape + (1, 128)))
    y_sample, _ = _run_group(x_sample, mods[:, :dec_b], (cache_k, cache_v), init, p, tiles)

    return (y_prompt, y_sample, jnp.stack(ex["k"], axis=1), jnp.stack(ex["v"], axis=1),
            jnp.stack(ex["C"], axis=1), jnp.stack(ex["n"], axis=1), jnp.stack(ex["m"], axis=1))
```

```python
import functools
import math

import jax
import jax.numpy as jnp
from jax import lax
from jax.experimental import pallas as pl
from jax.experimental.pallas import tpu as pltpu

F32 = jnp.float32
BF16 = jnp.bfloat16

EPS = 1e-6
ROPE_BASE = 10000.0
GRID_W = 64
N_MOD = 6
A_HEADS = 4
A_QK = 64
A_V = 2 * A_QK
A_WIDTH = A_HEADS * A_V
N_FREQ = A_QK // 4
B_GROUPS = 4
CHUNK = 128
SCAN_T = 256
C_HEADS = 4
LOG2E = 1.4426950408889634

HALO = 8
VMEM_LIMIT = 56 * 1024 * 1024
NEG_BIG = -1e30


def _cparams(*sem):
    return pltpu.CompilerParams(dimension_semantics=sem, vmem_limit_bytes=VMEM_LIMIT)


def _gelu(x):
    return 0.5 * x * (1.0 + jnp.tanh(math.sqrt(2.0 / math.pi) * (x + 0.044715 * (x * x * x))))


def _sigmoid(x):
    return 1.0 / (1.0 + jnp.exp(-x))


def _log_sigmoid(x):
    return jnp.minimum(x, 0.0) - jnp.log1p(jnp.exp(-jnp.abs(x)))


def _rms(x, w):
    return x * lax.rsqrt(jnp.mean(x * x, axis=-1, keepdims=True) + EPS) * w


def _modulated(x, nw, shift, scale):
    return _rms(x, nw) * (1.0 + scale) + shift


def _dot(a, b):
    return jnp.dot(a, b, preferred_element_type=F32)


def _conv3(xe, tm, seq_len, row0, w_ref, b_ref):
    n = xe.shape[0]
    prev = pltpu.roll(xe, 1, 0)[HALO:HALO + tm]
    nxt = pltpu.roll(xe, n - 1, 0)[HALO:HALO + tm]
    cur = xe[HALO:HALO + tm]
    pos = (row0 + lax.broadcasted_iota(jnp.int32, (tm, 1), 0)) % seq_len
    prev = jnp.where(pos == 0, 0.0, prev)
    nxt = jnp.where(pos == seq_len - 1, 0.0, nxt)
    return prev * w_ref[0:1, :] + cur * w_ref[1:2, :] + nxt * w_ref[2:3, :] + b_ref[...]


def _mod_body(c_ref, w_ref, b_ref, o_ref):
    c = c_ref[...]
    s = (c * _sigmoid(c)).astype(BF16)
    o_ref[0] = _dot(s, w_ref[0].astype(BF16)) + b_ref[0]


def _modulation(cond, w_mod, b_mod):
    depth, d, n = w_mod.shape
    r = cond.shape[0]
    tn = n // 4
    return pl.pallas_call(
        _mod_body,
        grid=(depth, n // tn),
        in_specs=[pl.BlockSpec((r, d), lambda l, j: (0, 0)),
                  pl.BlockSpec((1, d, tn), lambda l, j: (l, 0, j)),
                  pl.BlockSpec((1, 1, tn), lambda l, j: (l, 0, j))],
        out_specs=pl.BlockSpec((1, r, tn), lambda l, j: (l, 0, j)),
        out_shape=jax.ShapeDtypeStruct((depth, r, n), F32),
        compiler_params=_cparams("parallel", "parallel"),
        name="modulation",
    )(cond, w_mod, b_mod.reshape(depth, 1, n))


def _transposed_chunks(x, o_ref):
    for r in range(x.shape[0] // CHUNK):
        o_ref[r] = x[r * CHUNK:(r + 1) * CHUNK].T.astype(o_ref.dtype)


def _chunk_gmlp(gu, gv, gnw_ref, ws_ref, bs_ref):
    tm, bw = gu.shape
    ch = bw // B_GROUPS
    nch = tm // CHUNK
    vn = _rms(_gelu(gv), gnw_ref[...]).astype(BF16)
    parts = []
    for g in range(B_GROUPS):
        rhs = jnp.concatenate([vn[r * CHUNK:(r + 1) * CHUNK, g * ch:(g + 1) * ch] for r in range(nch)], axis=1)
        parts.append(_dot(ws_ref[g], rhs) + bs_ref[:, g:g + 1])
    s = jnp.concatenate(
        [jnp.concatenate([parts[g][:, r * ch:(r + 1) * ch] for g in range(B_GROUPS)], axis=1) for r in range(nch)],
        axis=0)
    return _gelu(gu) * s


def _in_proj0_body(*refs, rope, emit_kv):
    it = iter(refs)
    x_ref, nw_ref, mod_ref, w_ref, gnw_ref, ws_ref, bs_ref = [next(it) for _ in range(7)]
    tabs = [next(it) for _ in range(3)] if rope else None
    qt_ref, k_ref, vt_ref, g_ref = [next(it) for _ in range(4)]
    h = _modulated(x_ref[...], nw_ref[...], mod_ref[0, 0:1, :], mod_ref[0, 1:2, :])
    z = _dot(h.astype(BF16), w_ref[...])
    w = k_ref.shape[1]
    q, k, v = z[:, 0:w], z[:, w:2 * w], z[:, 2 * w:3 * w]
    if emit_kv:
        next(it)[...] = k
        next(it)[...] = v
    if rope:
        reps = w // tabs[0].shape[1]
        cos, sa, sb = [jnp.concatenate([tab[...]] * reps, axis=1) for tab in tabs]

        def rot(x):
            return x * cos + pltpu.roll(x, w - N_FREQ, 1) * sa + pltpu.roll(x, N_FREQ, 1) * sb

        q, k = rot(q), rot(k)
    _transposed_chunks(q * (A_QK ** -0.5 * LOG2E), qt_ref)
    k_ref[...] = k.astype(BF16)
    _transposed_chunks(v, vt_ref)
    bw = g_ref.shape[1]
    g_ref[...] = _chunk_gmlp(z[:, 3 * w:3 * w + bw], z[:, 3 * w + bw:], gnw_ref, ws_ref, bs_ref).astype(g_ref.dtype)


def _in_proj0(x, nw, mod, w_in, gnw, ws, bs_t, rope_tabs, emit_kv, rows_per_mod, seq_len, tm):
    m, d = x.shape
    n = w_in.shape[1]
    w = A_WIDTH
    bw = gnw.shape[0]
    rope = rope_tabs is not None
    row = lambda width: pl.BlockSpec((tm, width), lambda i: (i, 0))
    in_specs = [row(d),
                pl.BlockSpec((1, d), lambda i: (0, 0)),
                pl.BlockSpec((1, N_MOD, d), lambda i: ((i * tm) // rows_per_mod, 0, 0)),
                pl.BlockSpec((d, n), lambda i: (0, 0)),
                pl.BlockSpec((1, bw), lambda i: (0, 0)),
                pl.BlockSpec(ws.shape, lambda i: (0, 0, 0)),
                pl.BlockSpec(bs_t.shape, lambda i: (0, 0))]
    args = [x, nw.reshape(1, d), mod, w_in, gnw.reshape(1, bw), ws, bs_t]
    if rope:
        blocks = seq_len // tm
        in_specs += [pl.BlockSpec((tm, rope_tabs[0].shape[1]), lambda i: (i % blocks, 0))] * 3
        args += list(rope_tabs)
    chunks = pl.BlockSpec((tm // CHUNK, w, CHUNK), lambda i: (i, 0, 0))
    out_specs = [chunks, row(w), chunks, row(bw)]
    out_shape = [jax.ShapeDtypeStruct((m // CHUNK, w, CHUNK), BF16), jax.ShapeDtypeStruct((m, w), BF16),
                 jax.ShapeDtypeStruct((m // CHUNK, w, CHUNK), BF16), jax.ShapeDtypeStruct((m, bw), BF16)]
    if emit_kv:
        out_specs += [row(w), row(w)]
        out_shape += [jax.ShapeDtypeStruct((m, w), F32)] * 2
    return pl.pallas_call(
        functools.partial(_in_proj0_body, rope=rope, emit_kv=emit_kv),
        grid=(m // tm,),
        in_specs=in_specs,
        out_specs=out_specs,
        out_shape=out_shape,
        compiler_params=_cparams("parallel"),
        name="in_proj_even",
    )(*args)


def _ctx_prep_body(k_ref, v_ref, ko_ref, vt_ref):
    ko_ref[0] = k_ref[0].astype(BF16)
    _transposed_chunks(v_ref[0], vt_ref)


def _ctx_prep(ctx_k, ctx_v):
    b, p, w = ctx_k.shape
    return pl.pallas_call(
        _ctx_prep_body,
        grid=(b,),
        in_specs=[pl.BlockSpec((1, p, w), lambda bi: (bi, 0, 0))] * 2,
        out_specs=[pl.BlockSpec((1, p, w), lambda bi: (bi, 0, 0)),
                   pl.BlockSpec((p // CHUNK, w, CHUNK), lambda bi: (bi, 0, 0))],
        out_shape=[jax.ShapeDtypeStruct((b, p, w), BF16), jax.ShapeDtypeStruct((b * p // CHUNK, w, CHUNK), BF16)],
        compiler_params=_cparams("parallel"),
        name="ctx_prep",
    )(ctx_k, ctx_v)


def _joined(ref, first, count):
    return jnp.concatenate([ref[first + r] for r in range(count)], axis=1)


def _attn_body(*refs, has_ctx, tk, lam_init):
    if has_ctx:
        qt_ref, k_ref, vt_ref, ck_ref, cvt_ref, lam_ref, sw_ref, o_ref = refs
    else:
        qt_ref, k_ref, vt_ref, lam_ref, sw_ref, o_ref = refs
    qt = _joined(qt_ref, 0, qt_ref.shape[0])
    tq = qt.shape[1]
    sub = lax.broadcasted_iota(jnp.int32, qt.shape, 0)
    zero = jnp.zeros_like(qt)
    qq = jnp.concatenate([jnp.where(sub < A_QK, qt, zero), jnp.where(sub >= A_QK, qt, zero)], axis=1)

    m = jnp.full((1, 2 * tq), NEG_BIG, F32)
    l = jnp.zeros((1, 2 * tq), F32)
    acc = jnp.zeros((A_V, 2 * tq), F32)

    sources = ([(ck_ref, cvt_ref)] if has_ctx else []) + [(k_ref, vt_ref)]
    for k_src, vt_src in sources:
        lk = k_src.shape[1]
        ck = _pick_tile(lk, tk)
        per = ck // CHUNK
        for c in range(lk // ck):
            s = _dot(k_src[0, c * ck:(c + 1) * ck, :], qq)
            m_new = jnp.maximum(m, jnp.max(s, axis=0, keepdims=True))
            alpha = jnp.exp2(m - m_new)
            p = jnp.exp2(s - m_new)
            l = alpha * l + jnp.sum(p, axis=0, keepdims=True)
            acc = alpha * acc + _dot(_joined(vt_src, c * per, per), p.astype(BF16))
            m = m_new

    o = acc / l
    lam = (jnp.exp(jnp.sum(lam_ref[0:1, :] * lam_ref[1:2, :], axis=-1, keepdims=True))
           - jnp.exp(jnp.sum(lam_ref[2:3, :] * lam_ref[3:4, :], axis=-1, keepdims=True)) + lam_init)
    a = o[:, 0:tq] - lam * o[:, tq:]
    an = a * lax.rsqrt(jnp.mean(a * a, axis=0, keepdims=True) + EPS) * sw_ref[...] * (1.0 - lam_init)
    o_ref[0] = an.T.astype(o_ref.dtype)


def _attention(qt, k, vt, ctx, lam4, subln_w, lam_init, tq, tk):
    b, l, w = k.shape
    has_ctx = ctx is not None
    k_spec = lambda n: pl.BlockSpec((1, n, A_V), lambda bi, h, i: (bi, 0, h))
    vt_spec = lambda n: pl.BlockSpec((n // CHUNK, A_V, CHUNK), lambda bi, h, i: (bi, h, 0))
    nq = l // tq
    in_specs = [pl.BlockSpec((tq // CHUNK, A_V, CHUNK), lambda bi, h, i: (bi * nq + i, h, 0)), k_spec(l), vt_spec(l)]
    args = [qt, k, vt]
    if has_ctx:
        p = ctx[0].shape[1]
        in_specs += [k_spec(p), vt_spec(p)]
        args += list(ctx)
    in_specs += [pl.BlockSpec((4, A_QK), lambda bi, h, i: (0, 0)),
                 pl.BlockSpec((A_V, 1), lambda bi, h, i: (0, 0))]
    args += [lam4, subln_w.reshape(A_V, 1)]
    return pl.pallas_call(
        functools.partial(_attn_body, has_ctx=has_ctx, tk=tk, lam_init=lam_init),
        grid=(b, A_HEADS, l // tq),
        in_specs=in_specs,
        out_specs=pl.BlockSpec((1, tq, A_V), lambda bi, h, i: (bi, i, h)),
        out_shape=jax.ShapeDtypeStruct((b, l, w), BF16),
        compiler_params=_cparams("parallel", "parallel", "parallel"),
        name="diff_attention",
    )(*args)


def _mlstm_mix(hf, hb, og, xc, hn_ref, sk_ref, wo_ref):
    hd = hn_ref.shape[1] // C_HEADS
    hs = hf + hb
    hn = jnp.concatenate([_rms(hs[:, h * hd:(h + 1) * hd], hn_ref[:, h * hd:(h + 1) * hd]) for h in range(C_HEADS)],
                         axis=1)
    return _dot((_sigmoid(og) * (hn + sk_ref[...] * xc)).astype(BF16), wo_ref[...])


def _even_mix(a, g, wo_ref):
    aw = a.shape[1]
    return _dot(a.astype(BF16), wo_ref[0:aw, :]) + _dot(g.astype(BF16), wo_ref[aw:, :])


def _ffn_body(*refs, seq_len, fc, final, n_acts):
    it = iter(refs)
    x_ref, xp_ref, xn_ref, nw_ref, mod_ref, wa_ref, wg_ref, cw_ref, cb_ref, wd_ref = [next(it) for _ in range(10)]
    fw_ref = next(it) if final else None
    acts = [[next(it) for _ in range(3)] for _ in range(n_acts)]
    params = [next(it) for _ in range(3 if n_acts == 4 else 1)]
    o_ref = next(it)
    tm, d = x_ref.shape
    dff = wa_ref.shape[1]
    xe = jnp.concatenate([xp_ref[...], x_ref[...], xn_ref[...]], axis=0)
    ext = [jnp.concatenate([prev[...].astype(F32)[-HALO:], main[...].astype(F32), nxt[...].astype(F32)[:HALO]], axis=0)
           for main, prev, nxt in acts]
    xe = xe + mod_ref[0, 2:3, :] * (_mlstm_mix(*ext, *params) if n_acts == 4 else _even_mix(*ext, *params))
    x = xe[HALO:HALO + tm]
    he = _modulated(xe, nw_ref[...], mod_ref[0, 3:4, :], mod_ref[0, 4:5, :]).astype(BF16)
    hc = he[HALO:HALO + tm]
    row0 = pl.program_id(0) * tm
    acc = jnp.zeros((tm, d), F32)
    for c in range(dff // fc):
        cs = slice(c * fc, (c + 1) * fc)
        ae = _dot(he, wa_ref[:, cs])
        g = _dot(hc, wg_ref[:, cs])
        conv = _conv3(ae, tm, seq_len, row0, cw_ref.at[:, cs], cb_ref.at[:, cs])
        acc = acc + _dot((_gelu(conv) * g).astype(BF16), wd_ref[cs, :])
    out = x + mod_ref[0, 5:6, :] * acc
    if final:
        out = _rms(out, fw_ref[...])
    o_ref[...] = out


def _halo_specs(tm, m, width, rows=HALO):
    nb = tm // rows
    last = m // rows - 1
    return [pl.BlockSpec((rows, width), lambda i: (jnp.maximum(i * nb - 1, 0), 0)),
            pl.BlockSpec((rows, width), lambda i: (jnp.minimum((i + 1) * nb, last), 0))]


def _conv_ffn(x, nw, mod, wa, wg, cw, cb, wd, fw, premix, rows_per_mod, seq_len, tm, fc):
    m, d = x.shape
    dff = wa.shape[1]
    final = fw is not None
    const = lambda i: (0, 0)
    in_specs = ([pl.BlockSpec((tm, d), lambda i: (i, 0))] + _halo_specs(tm, m, d)
                + [pl.BlockSpec((1, d), const),
                   pl.BlockSpec((1, N_MOD, d), lambda i: ((i * tm) // rows_per_mod, 0, 0)),
                   pl.BlockSpec((d, dff), const),
                   pl.BlockSpec((d, dff), const),
                   pl.BlockSpec((3, dff), const),
                   pl.BlockSpec((1, dff), const),
                   pl.BlockSpec((dff, d), const)])
    args = [x, x, x, nw.reshape(1, d), mod, wa, wg, cw, cb.reshape(1, dff), wd]
    if final:
        in_specs.append(pl.BlockSpec((1, d), const))
        args.append(fw.reshape(1, d))
    acts, params = premix
    for act in acts:
        in_specs += [pl.BlockSpec((tm, act.shape[1]), lambda i: (i, 0))] + _halo_specs(tm, m, act.shape[1], 2 * HALO)
        args += [act, act, act]
    for prm in params:
        prm = prm.reshape(1, -1) if prm.ndim == 1 else prm
        in_specs.append(pl.BlockSpec(prm.shape, const))
        args.append(prm)
    return pl.pallas_call(
        functools.partial(_ffn_body, seq_len=seq_len, fc=fc, final=final, n_acts=len(acts)),
        grid=(m // tm,),
        in_specs=in_specs,
        out_specs=pl.BlockSpec((tm, d), lambda i: (i, 0)),
        out_shape=jax.ShapeDtypeStruct((m, d), F32),
        compiler_params=_cparams("parallel"),
        name="conv_ffn",
    )(*args)


def _in_proj1_body(x_ref, xp_ref, xn_ref, nw_ref, mod_ref, wm_ref, wr_ref, cw_ref, cb_ref, wq_ref, wk_ref, wv_ref,
                   q_ref, kt_ref, v_ref, xc_ref, og_ref, gt_ref, *, seq_len):
    tm = x_ref.shape[0]
    ci = wm_ref.shape[1]
    hd = ci // C_HEADS
    xe = jnp.concatenate([xp_ref[...], x_ref[...], xn_ref[...]], axis=0)
    he = _modulated(xe, nw_ref[...], mod_ref[0, 0:1, :], mod_ref[0, 1:2, :]).astype(BF16)
    xme = _dot(he, wm_ref[...])
    rest = _dot(he[HALO:HALO + tm], wr_ref[...])
    og_ref[...] = rest[:, 0:ci].astype(BF16)
    gt_ref[...] = rest[:, ci:]
    conv = _conv3(xme, tm, seq_len, pl.program_id(0) * tm, cw_ref, cb_ref)
    xc = conv * _sigmoid(conv)
    xcb = xc.astype(BF16)
    xc_ref[...] = xcb
    xmb = xme[HALO:HALO + tm].astype(BF16)
    for h in range(C_HEADS):
        hs = slice(h * hd, (h + 1) * hd)
        q_ref[:, hs] = _dot(xcb[:, hs], wq_ref[h]).astype(BF16)
        k = _dot(xcb[:, hs], wk_ref[h]) * hd ** -0.5
        for r in range(tm // SCAN_T):
            kt_ref[r, hs, :] = k[r * SCAN_T:(r + 1) * SCAN_T].T.astype(BF16)
        v_ref[:, hs] = _dot(xmb[:, hs], wv_ref[h]).astype(BF16)


def _in_proj1(x, nw, mod, wm, wr, cw, cb, wq, wk, wv, rows_per_mod, seq_len, tm):
    m, d = x.shape
    ci = wm.shape[1]
    ng = wr.shape[1] - ci
    const = lambda i: (0, 0)
    const3 = lambda i: (0, 0, 0)
    row = pl.BlockSpec((tm, ci), lambda i: (i, 0))
    return pl.pallas_call(
        functools.partial(_in_proj1_body, seq_len=seq_len),
        grid=(m // tm,),
        in_specs=[pl.BlockSpec((tm, d), lambda i: (i, 0))] + _halo_specs(tm, m, d)
        + [pl.BlockSpec((1, d), const),
           pl.BlockSpec((1, N_MOD, d), lambda i: ((i * tm) // rows_per_mod, 0, 0)),
           pl.BlockSpec(wm.shape, const), pl.BlockSpec(wr.shape, const),
           pl.BlockSpec((3, ci), const), pl.BlockSpec((1, ci), const),
           pl.BlockSpec(wq.shape, const3), pl.BlockSpec(wk.shape, const3), pl.BlockSpec(wv.shape, const3)],
        out_specs=[row, pl.BlockSpec((tm // SCAN_T, ci, SCAN_T), lambda i: (i, 0, 0)), row, row, row,
                   pl.BlockSpec((tm, ng), lambda i: (i, 0))],
        out_shape=[jax.ShapeDtypeStruct((m, ci), BF16), jax.ShapeDtypeStruct((m // SCAN_T, ci, SCAN_T), BF16)]
        + [jax.ShapeDtypeStruct((m, ci), BF16)] * 3 + [jax.ShapeDtypeStruct((m, ng), F32)],
        compiler_params=_cparams("parallel"),
        name="in_proj_odd",
    )(x, x, x, nw.reshape(1, d), mod, wm, wr, cw, cb.reshape(1, ci), wq, wk, wv)


def _split3(x):
    hi = x.astype(BF16)
    r1 = x - hi.astype(F32)
    mid = r1.astype(BF16)
    lo = (r1 - mid.astype(F32)).astype(BF16)
    return hi, mid, lo


def _mlstm_scan_body(*refs, zero_init, cpb):
    it = iter(refs)
    fwd = [next(it) for _ in range(4)]
    bwd = [next(it) for _ in range(4)]
    bg_ref = next(it)
    init = None if zero_init else [next(it) for _ in range(3)]
    hf_ref, hb_ref, c_ref, n_ref, m_ref, cx_ref = [next(it) for _ in range(6)]
    t = SCAN_T
    hd = c_ref.shape[-1]
    nw = cx_ref.shape[-1] - hd
    wide = lambda a: jnp.concatenate([a] * (hd // nw), axis=1)
    j = pl.program_id(1)

    @pl.when(j == 0)
    def _():
        for dr in range(2):
            for h in range(C_HEADS):
                if zero_init:
                    cx_ref[dr, h] = jnp.zeros(cx_ref.shape[2:], F32)
                else:
                    cx_ref[dr, h, :, 0:hd] = init[0][0, dr, h]
                    cx_ref[dr, h, :, hd:] = jnp.broadcast_to(init[1][0, dr, h], (nw, hd)).T
        m_ref[...] = jnp.zeros_like(m_ref) if zero_init else init[2][...]

    rr = lax.broadcasted_iota(jnp.int32, (t, t), 0)
    cc = lax.broadcasted_iota(jnp.int32, (t, t), 1)
    ones = jnp.ones((t, nw), BF16)

    srcs = ((fwd, hf_ref), (bwd, hb_ref))
    chunk_of = lambda ci, dr: ci if dr == 0 else cpb - 1 - ci

    m_cur = {(dr, h): m_ref[0, dr, h][:, 0:1] for dr in range(2) for h in range(C_HEADS)}
    pre = {}
    for ci in range(cpb):
        for dr in range(2):
            g_ref = srcs[dr][0][3]
            ch = chunk_of(ci, dr)
            mask = (rr >= cc) if dr == 0 else (rr <= cc)
            tri = jnp.where(mask, 1.0, 0.0).astype(BF16)
            gates = g_ref[0, ch * t:(ch + 1) * t, :] + bg_ref[...]
            hi, mid, lo = _split3(_log_sigmoid(gates))
            bsum = _dot(tri, hi) + _dot(tri, mid) + _dot(tri, lo)
            bsum_t = bsum.T
            gates_t = gates.T
            for h in range(C_HEADS):
                col_i = 2 * dr * C_HEADS + h
                col_f = (2 * dr + 1) * C_HEADS + h
                b_col = jnp.broadcast_to(bsum[:, col_f:col_f + 1], (t, t))
                b_row = bsum_t[col_f:col_f + 1, :]
                i_row = gates_t[col_i:col_i + 1, :]
                m_prev = m_cur[dr, h]
                a_inter = b_col + m_prev
                dmat = jnp.where(mask, b_col - b_row + i_row, -jnp.inf)
                m_t = jnp.maximum(a_inter, jnp.max(dmat, axis=1, keepdims=True))
                b_end = b_row[:, t - 1:t] if dr == 0 else b_row[:, 0:1]
                g_row = b_end - b_row + i_row
                m_new = jnp.maximum(b_end + m_prev, jnp.max(g_row, axis=1, keepdims=True))
                m_cur[dr, h] = m_new
                pre[ci, dr, h] = (jnp.where(mask, jnp.exp(dmat - m_t), 0.0),
                                  jnp.exp(a_inter - m_t)[:, 0:nw],
                                  jnp.exp(-m_t)[:, 0:nw],
                                  jnp.exp(b_end + m_prev - m_new),
                                  jnp.exp(g_row - m_new))

    for ci in range(cpb):
        for dr in range(2):
            (q_ref, kt_ref, v_ref, _), h_out = srcs[dr]
            ch = chunk_of(ci, dr)
            rows = slice(ch * t, (ch + 1) * t)
            for h in range(C_HEADS):
                decay_w, w_inter, floor, decay, w_row = pre[ci, dr, h]
                hs = slice(h * hd, (h + 1) * hd)
                cx = cx_ref[dr, h]
                qc = q_ref[0, rows, hs]
                v1 = jnp.concatenate([v_ref[0, rows, hs], ones], axis=1)
                ktc = kt_ref[ch, hs, :]
                s = _dot(qc, ktc) * decay_w
                intra = _dot(s.astype(BF16), v1)
                inter = _dot(qc, cx.astype(BF16))
                den = intra[:, hd:] + w_inter * inter[:, hd:]
                inv = 1.0 / jnp.maximum(jnp.abs(den), floor)
                num = intra[:, 0:hd] + wide(w_inter) * inter[:, 0:hd]
                h_out[0, rows, hs] = (num * wide(inv)).astype(h_out.dtype)
                kw_t = (ktc.astype(F32) * w_row).astype(BF16)
                cx_ref[dr, h] = decay * cx + _dot(kw_t, v1)

    for (dr, h), m_new in m_cur.items():
        m_ref[0, dr, h] = jnp.broadcast_to(m_new, (1, m_ref.shape[-1]))

    @pl.when(j == pl.num_programs(1) - 1)
    def _():
        for dr in range(2):
            for h in range(C_HEADS):
                c_ref[0, dr, h] = cx_ref[dr, h, :, 0:hd]
                n_ref[0, dr, h] = cx_ref[dr, h, :, hd:].T[0:1, :]


def _mlstm_scan(q, kt, v, gates, bg, init, lb):
    b, l, ci = q.shape
    hd = ci // C_HEADS
    nb = l // lb
    cpb = lb // SCAN_T
    zero_init = init is None
    in_specs, args = [], []
    for blk in (lambda j: j, lambda j: nb - 1 - j):
        row = pl.BlockSpec((1, lb, ci), lambda bi, j, blk=blk: (bi, blk(j), 0))
        in_specs += [row,
                     pl.BlockSpec((cpb, ci, SCAN_T), lambda bi, j, blk=blk: (bi * nb + blk(j), 0, 0)),
                     row,
                     pl.BlockSpec((1, lb, gates.shape[2]), lambda bi, j, blk=blk: (bi, blk(j), 0))]
        args += [q, kt, v, gates]
    in_specs.append(pl.BlockSpec((1, 128), lambda bi, j: (0, 0)))
    args.append(bg)
    st = lambda *tail: pl.BlockSpec((1, 2, C_HEADS) + tail, lambda bi, j: (bi, 0, 0, 0, 0))
    states = [st(hd, hd), st(1, hd), st(1, 128)]
    if not zero_init:
        in_specs += states
        args += list(init)
    return pl.pallas_call(
        functools.partial(_mlstm_scan_body, zero_init=zero_init, cpb=cpb),
        grid=(b, nb),
        in_specs=in_specs,
        out_specs=[pl.BlockSpec((1, lb, ci), lambda bi, j: (bi, j, 0)),
                   pl.BlockSpec((1, lb, ci), lambda bi, j: (bi, nb - 1 - j, 0))] + states,
        out_shape=[jax.ShapeDtypeStruct((b, l, ci), BF16),
                   jax.ShapeDtypeStruct((b, l, ci), BF16),
                   jax.ShapeDtypeStruct((b, 2, C_HEADS, hd, hd), F32),
                   jax.ShapeDtypeStruct((b, 2, C_HEADS, 1, hd), F32),
                   jax.ShapeDtypeStruct((b, 2, C_HEADS, 1, 128), F32)],
        scratch_shapes=[pltpu.VMEM((2, C_HEADS, hd, hd + CHUNK), F32)],
        compiler_params=_cparams("parallel", "arbitrary"),
        name="mlstm_scan",
    )(*args)


def _rope_tables(length):
    n_rows = length // GRID_W
    rows = jnp.repeat(jnp.arange(n_rows, dtype=F32), GRID_W)
    cols = jnp.tile(jnp.arange(GRID_W, dtype=F32), n_rows)
    inv = ROPE_BASE ** (-jnp.arange(N_FREQ, dtype=F32) / N_FREQ)
    ang = jnp.stack([rows[:, None] * inv, cols[:, None] * inv], axis=1)
    cos, sin = jnp.cos(ang), jnp.sin(ang)
    zeros = jnp.zeros_like(sin)
    lanes = lambda first, second: jnp.tile(
        jnp.stack([first, second], axis=2).reshape(length, A_QK), (1, A_V // A_QK))
    return lanes(cos, cos), lanes(-sin, zeros), lanes(zeros, sin)


def _pick_tile(n, want):
    t = min(n, want)
    while n % t:
        t //= 2
    return t


def _run_group(x3, mods, ctx, init, p, tiles):
    b, l, d = x3.shape
    m = b * l
    rows_per_mod = l if mods.shape[1] > 1 else m
    tm = _pick_tile(l if mods.shape[1] > 1 else m, tiles["tm"])
    x = x3.reshape(m, d)
    extras = {}
    depth = mods.shape[0]
    for layer in range(depth):
        mod = mods[layer]
        if layer % 2 == 0:
            e = layer // 2
            lam_init = 0.8 - 0.6 * math.exp(-0.3 * layer)
            rope = _rope_tables(l) if ctx is not None else None
            outs = _in_proj0(x, p["norm1_w"][layer], mod, p["w_in0"][e], p["gate_norm_w"][e], p["w_spatial"][e],
                             p["b_spatial"][e].T, rope, ctx is None, rows_per_mod, l, tm)
            qt, k, vt, g = outs[:4]
            ctx_e = _ctx_prep(ctx[0][:, e].reshape(b, -1, A_WIDTH), ctx[1][:, e].reshape(b, -1, A_WIDTH)) \
                if ctx is not None else None
            lam4 = jnp.stack([p["lam_q1"][e], p["lam_k1"][e], p["lam_q2"][e], p["lam_k2"][e]])
            a = _attention(qt, k.reshape(b, l, A_WIDTH), vt, ctx_e, lam4, p["subln_w"][e],
                           lam_init, _pick_tile(l, tiles["tq"]), tiles["tk"])
            premix = ((a.reshape(m, A_WIDTH), g), (p["w_out0"][e],))
            for name, kv in zip("kv", outs[4:]):
                extras.setdefault(name, []).append(kv.reshape(b, l, A_HEADS, A_V))
        else:
            o = layer // 2
            ci = p["w_out1"].shape[1]
            q, kt, v, xc, og, gates = _in_proj1(x, p["norm1_w"][layer], mod, p["w_in1_m"][o], p["w_in1_r"][o],
                                                p["mconv_w"][o], p["mconv_b"][o], p["w_q"][o], p["w_k"][o],
                                                p["w_v"][o], rows_per_mod, l, tm)
            init_o = None if init is None else tuple(s[:, o] for s in init)
            hf, hb, c_fin, n_fin, m_fin = _mlstm_scan(q.reshape(b, l, ci), kt, v.reshape(b, l, ci),
                                                      gates.reshape(b, l, -1), p["b_gates"][o], init_o,
                                                      _pick_tile(l, tiles["lb"]))
            premix = ((hf.reshape(m, ci), hb.reshape(m, ci), og, xc),
                      (p["head_norm_w"][o], p["skip_w"][o], p["w_out1"][o]))
            extras.setdefault("C", []).append(c_fin)
            extras.setdefault("n", []).append(n_fin[:, :, :, 0, :])
            extras.setdefault("m", []).append(m_fin[:, :, :, 0, 0])
        fw = p["final_norm_w"] if layer == depth - 1 else None
        x = _conv_ffn(x, p["norm2_w"][layer], mod, p["w_up_a"][layer], p["w_up_g"][layer], p["fconv_w"][layer],
                      p["fconv_b"][layer], p["w_down"][layer], fw, premix, rows_per_mod, l, tm, tiles["fc"])
    return x.reshape(b, l, d), extras


def kernel(x_prompt, x_sample, cache_k, cache_v, state_C, state_n, state_m, c, c_ctx, w_mod, b_mod, norm1_w, norm2_w,
           w_in0, lam_q1, lam_k1, lam_q2, lam_k2, subln_w, gate_norm_w, w_spatial, b_spatial, w_out0, w_in1,
           b_gates, mconv_w, mconv_b, w_q, w_k, w_v, head_norm_w, skip_w, w_out1, w_up, fconv_w, fconv_b, w_down,
           final_norm_w):
    depth, d, _ = w_mod.shape
    dec_b = x_sample.shape[0]
    dff = w_down.shape[1]
    ci = w_out1.shape[1]

    n_cond = -(-(dec_b + 1) // 8) * 8
    cond = jnp.zeros((n_cond, d), F32).at[:dec_b].set(c).at[dec_b].set(c_ctx)
    mods = _modulation(cond, w_mod, b_mod).reshape(depth, n_cond, N_MOD, d)

    n_in1 = w_in1.shape[2]
    pad1 = -(-n_in1 // 128) * 128 - n_in1
    p = dict(
        norm1_w=norm1_w, norm2_w=norm2_w, lam_q1=lam_q1, lam_k1=lam_k1, lam_q2=lam_q2, lam_k2=lam_k2,
        subln_w=subln_w, gate_norm_w=gate_norm_w, b_spatial=b_spatial, mconv_w=mconv_w, mconv_b=mconv_b,
        head_norm_w=head_norm_w, skip_w=skip_w, fconv_w=fconv_w, fconv_b=fconv_b, final_norm_w=final_norm_w,
        w_in0=w_in0.astype(BF16), w_spatial=w_spatial.astype(BF16), w_out0=w_out0.astype(BF16),
        w_in1_m=w_in1[:, :, :ci].astype(BF16),
        w_in1_r=jnp.pad(w_in1[:, :, ci:], ((0, 0), (0, 0), (0, pad1))).astype(BF16),
        b_gates=jnp.pad(b_gates, ((0, 0), (0, 128 - b_gates.shape[1])))[:, None, :],
        w_q=w_q.astype(BF16), w_k=w_k.astype(BF16), w_v=w_v.astype(BF16), w_out1=w_out1.astype(BF16),
        w_up_a=w_up[:, :, :dff].astype(BF16), w_up_g=w_up[:, :, dff:].astype(BF16), w_down=w_down.astype(BF16),
    )
    tiles = dict(tm=512, tq=1024, tk=512, fc=2816, lb=512)

    y_prompt, ex = _run_group(x_prompt, mods[:, dec_b:dec_b + 1], None, None, p, tiles)
    init = (state_C.astype(F32),
            state_n.astype(F32)[:, :, :, :, None, :],
            jnp.broadcast_to(state_m.astype(F32)[:, :, :, :, None, None], state_m.shape + (1, 128)))
    y_sample, _ = _run_group(x_sample, mods[:, :dec_b], (cache_k, cache_v), init, p, tiles)

    return (y_prompt, y_sample, jnp.stack(ex["k"], axis=1), jnp.stack(ex["v"], axis=1),
            jnp.stack(ex["C"], axis=1), jnp.stack(ex["n"], axis=1), jnp.stack(ex["m"], axis=1))
```

```python
import functools
import math

import jax
import jax.numpy as jnp
from jax import lax
from jax.experimental import pallas as pl
from jax.experimental.pallas import tpu as pltpu

F32 = jnp.float32
BF16 = jnp.bfloat16

EPS = 1e-6
ROPE_BASE = 10000.0
GRID_W = 64
N_MOD = 6
A_HEADS = 4
A_QK = 64
A_V = 2 * A_QK
A_WIDTH = A_HEADS * A_V
N_FREQ = A_QK // 4
B_GROUPS = 4
CHUNK = 128
SCAN_T = 256
C_HEADS = 4
LOG2E = 1.4426950408889634

HALO = 8
VMEM_LIMIT = 56 * 1024 * 1024
NEG_BIG = -1e30


def _cparams(*sem):
    return pltpu.CompilerParams(dimension_semantics=sem, vmem_limit_bytes=VMEM_LIMIT)


def _gelu(x):
    return 0.5 * x * (1.0 + jnp.tanh(math.sqrt(2.0 / math.pi) * (x + 0.044715 * (x * x * x))))


def _sigmoid(x):
    return 1.0 / (1.0 + jnp.exp(-x))


def _log_sigmoid(x):
    return jnp.minimum(x, 0.0) - jnp.log1p(jnp.exp(-jnp.abs(x)))


def _rms(x, w):
    return x * lax.rsqrt(jnp.mean(x * x, axis=-1, keepdims=True) + EPS) * w


def _modulated(x, nw, shift, scale):
    return _rms(x, nw) * (1.0 + scale) + shift


def _dot(a, b):
    return jnp.dot(a, b, preferred_element_type=F32)


def _conv3(xe, tm, seq_len, row0, w_ref, b_ref):
    n = xe.shape[0]
    prev = pltpu.roll(xe, 1, 0)[HALO:HALO + tm]
    nxt = pltpu.roll(xe, n - 1, 0)[HALO:HALO + tm]
    cur = xe[HALO:HALO + tm]
    pos = (row0 + lax.broadcasted_iota(jnp.int32, (tm, 1), 0)) % seq_len
    prev = jnp.where(pos == 0, 0.0, prev)
    nxt = jnp.where(pos == seq_len - 1, 0.0, nxt)
    return prev * w_ref[0:1, :] + cur * w_ref[1:2, :] + nxt * w_ref[2:3, :] + b_ref[...]


def _mod_body(c_ref, w_ref, b_ref, o_ref):
    c = c_ref[...]
    s = (c * _sigmoid(c)).astype(BF16)
    o_ref[0] = _dot(s, w_ref[0].astype(BF16)) + b_ref[0]


def _modulation(cond, w_mod, b_mod):
    depth, d, n = w_mod.shape
    r = cond.shape[0]
    tn = n // 4
    return pl.pallas_call(
        _mod_body,
        grid=(depth, n // tn),
        in_specs=[pl.BlockSpec((r, d), lambda l, j: (0, 0)),
                  pl.BlockSpec((1, d, tn), lambda l, j: (l, 0, j)),
                  pl.BlockSpec((1, 1, tn), lambda l, j: (l, 0, j))],
        out_specs=pl.BlockSpec((1, r, tn), lambda l, j: (l, 0, j)),
        out_shape=jax.ShapeDtypeStruct((depth, r, n), F32),
        compiler_params=_cparams("parallel", "parallel"),
        name="modulation",
    )(cond, w_mod, b_mod.reshape(depth, 1, n))


def _transposed_chunks(x, o_ref):
    for r in range(x.shape[0] // CHUNK):
        o_ref[r] = x[r * CHUNK:(r + 1) * CHUNK].T.astype(o_ref.dtype)


def _chunk_gmlp(gu, gv, gnw_ref, ws_ref, bs_ref):
    tm, bw = gu.shape
    ch = bw // B_GROUPS
    nch = tm // CHUNK
    vn = _rms(_gelu(gv), gnw_ref[...]).astype(BF16)
    parts = []
    for g in range(B_GROUPS):
        rhs = jnp.concatenate([vn[r * CHUNK:(r + 1) * CHUNK, g * ch:(g + 1) * ch] for r in range(nch)], axis=1)
        parts.append(_dot(ws_ref[g], rhs) + bs_ref[:, g:g + 1])
    s = jnp.concatenate(
        [jnp.concatenate([parts[g][:, r * ch:(r + 1) * ch] for g in range(B_GROUPS)], axis=1) for r in range(nch)],
        axis=0)
    return _gelu(gu) * s


def _in_proj0_body(*refs, rope, emit_kv):
    it = iter(refs)
    x_ref, nw_ref, mod_ref, w_ref, gnw_ref, ws_ref, bs_ref = [next(it) for _ in range(7)]
    tabs = [next(it) for _ in range(3)] if rope else None
    qt_ref, k_ref, vt_ref, g_ref = [next(it) for _ in range(4)]
    h = _modulated(x_ref[...], nw_ref[...], mod_ref[0, 0:1, :], mod_ref[0, 1:2, :])
    z = _dot(h.astype(BF16), w_ref[...])
    w = k_ref.shape[1]
    q, k, v = z[:, 0:w], z[:, w:2 * w], z[:, 2 * w:3 * w]
    if emit_kv:
        next(it)[...] = k
        next(it)[...] = v
    if rope:
        reps = w // tabs[0].shape[1]
        cos, sa, sb = [jnp.concatenate([tab[...]] * reps, axis=1) for tab in tabs]

        def rot(x):
            return x * cos + pltpu.roll(x, w - N_FREQ, 1) * sa + pltpu.roll(x, N_FREQ, 1) * sb

        q, k = rot(q), rot(k)
    _transposed_chunks(q * (A_QK ** -0.5 * LOG2E), qt_ref)
    k_ref[...] = k.astype(BF16)
    _transposed_chunks(v, vt_ref)
    bw = g_ref.shape[1]
    g_ref[...] = _chunk_gmlp(z[:, 3 * w:3 * w + bw], z[:, 3 * w + bw:], gnw_ref, ws_ref, bs_ref).astype(g_ref.dtype)


def _in_proj0(x, nw, mod, w_in, gnw, ws, bs_t, rope_tabs, emit_kv, rows_per_mod, seq_len, tm):
    m, d = x.shape
    n = w_in.shape[1]
    w = A_WIDTH
    bw = gnw.shape[0]
    rope = rope_tabs is not None
    row = lambda width: pl.BlockSpec((tm, width), lambda i: (i, 0))
    in_specs = [row(d),
                pl.BlockSpec((1, d), lambda i: (0, 0)),
                pl.BlockSpec((1, N_MOD, d), lambda i: ((i * tm) // rows_per_mod, 0, 0)),
                pl.BlockSpec((d, n), lambda i: (0, 0)),
                pl.BlockSpec((1, bw), lambda i: (0, 0)),
                pl.BlockSpec(ws.shape, lambda i: (0, 0, 0)),
                pl.BlockSpec(bs_t.shape, lambda i: (0, 0))]
    args = [x, nw.reshape(1, d), mod, w_in, gnw.reshape(1, bw), ws, bs_t]
    if rope:
        blocks = seq_len // tm
        in_specs += [pl.BlockSpec((tm, rope_tabs[0].shape[1]), lambda i: (i % blocks, 0))] * 3
        args += list(rope_tabs)
    chunks = pl.BlockSpec((tm // CHUNK, w, CHUNK), lambda i: (i, 0, 0))
    out_specs = [chunks, row(w), chunks, row(bw)]
    out_shape = [jax.ShapeDtypeStruct((m // CHUNK, w, CHUNK), BF16), jax.ShapeDtypeStruct((m, w), BF16),
                 jax.ShapeDtypeStruct((m // CHUNK, w, CHUNK), BF16), jax.ShapeDtypeStruct((m, bw), BF16)]
    if emit_kv:
        out_specs += [row(w), row(w)]
        out_shape += [jax.ShapeDtypeStruct((m, w), F32)] * 2
    return pl.pallas_call(
        functools.partial(_in_proj0_body, rope=rope, emit_kv=emit_kv),
        grid=(m // tm,),
        in_specs=in_specs,
        out_specs=out_specs,
        out_shape=out_shape,
        compiler_params=_cparams("parallel"),
        name="in_proj_even",
    )(*args)


def _ctx_prep_body(k_ref, v_ref, ko_ref, vt_ref):
    ko_ref[0] = k_ref[0].astype(BF16)
    _transposed_chunks(v_ref[0], vt_ref)


def _ctx_prep(ctx_k, ctx_v):
    b, p, w = ctx_k.shape
    return pl.pallas_call(
        _ctx_prep_body,
        grid=(b,),
        in_specs=[pl.BlockSpec((1, p, w), lambda bi: (bi, 0, 0))] * 2,
        out_specs=[pl.BlockSpec((1, p, w), lambda bi: (bi, 0, 0)),
                   pl.BlockSpec((p // CHUNK, w, CHUNK), lambda bi: (bi, 0, 0))],
        out_shape=[jax.ShapeDtypeStruct((b, p, w), BF16), jax.ShapeDtypeStruct((b * p // CHUNK, w, CHUNK), BF16)],
        compiler_params=_cparams("parallel"),
        name="ctx_prep",
    )(ctx_k, ctx_v)


def _joined(ref, first, count):
    return jnp.concatenate([ref[first + r] for r in range(count)], axis=1)


def _attn_body(*refs, has_ctx, tk, lam_init):
    if has_ctx:
        qt_ref, k_ref, vt_ref, ck_ref, cvt_ref, lam_ref, sw_ref, o_ref = refs
    else:
        qt_ref, k_ref, vt_ref, lam_ref, sw_ref, o_ref = refs
    qt = _joined(qt_ref, 0, qt_ref.shape[0])
    tq = qt.shape[1]
    sub = lax.broadcasted_iota(jnp.int32, qt.shape, 0)
    zero = jnp.zeros_like(qt)
    qq = jnp.concatenate([jnp.where(sub < A_QK, qt, zero), jnp.where(sub >= A_QK, qt, zero)], axis=1)

    m = jnp.full((1, 2 * tq), NEG_BIG, F32)
    l = jnp.zeros((1, 2 * tq), F32)
    acc = jnp.zeros((A_V, 2 * tq), F32)

    sources = ([(ck_ref, cvt_ref)] if has_ctx else []) + [(k_ref, vt_ref)]
    for k_src, vt_src in sources:
        lk = k_src.shape[1]
        ck = _pick_tile(lk, tk)
        per = ck // CHUNK
        for c in range(lk // ck):
            s = _dot(k_src[0, c * ck:(c + 1) * ck, :], qq)
            m_new = jnp.maximum(m, jnp.max(s, axis=0, keepdims=True))
            alpha = jnp.exp2(m - m_new)
            p = jnp.exp2(s - m_new)
            l = alpha * l + jnp.sum(p, axis=0, keepdims=True)
            acc = alpha * acc + _dot(_joined(vt_src, c * per, per), p.astype(BF16))
            m = m_new

    o = acc / l
    lam = (jnp.exp(jnp.sum(lam_ref[0:1, :] * lam_ref[1:2, :], axis=-1, keepdims=True))
           - jnp.exp(jnp.sum(lam_ref[2:3, :] * lam_ref[3:4, :], axis=-1, keepdims=True)) + lam_init)
    a = o[:, 0:tq] - lam * o[:, tq:]
    an = a * lax.rsqrt(jnp.mean(a * a, axis=0, keepdims=True) + EPS) * sw_ref[...] * (1.0 - lam_init)
    o_ref[0] = an.T.astype(o_ref.dtype)


def _attention(qt, k, vt, ctx, lam4, subln_w, lam_init, tq, tk):
    b, l, w = k.shape
    has_ctx = ctx is not None
    k_spec = lambda n: pl.BlockSpec((1, n, A_V), lambda bi, h, i: (bi, 0, h))
    vt_spec = lambda n: pl.BlockSpec((n // CHUNK, A_V, CHUNK), lambda bi, h, i: (bi, h, 0))
    nq = l // tq
    in_specs = [pl.BlockSpec((tq // CHUNK, A_V, CHUNK), lambda bi, h, i: (bi * nq + i, h, 0)), k_spec(l), vt_spec(l)]
    args = [qt, k, vt]
    if has_ctx:
        p = ctx[0].shape[1]
        in_specs += [k_spec(p), vt_spec(p)]
        args += list(ctx)
    in_specs += [pl.BlockSpec((4, A_QK), lambda bi, h, i: (0, 0)),
                 pl.BlockSpec((A_V, 1), lambda bi, h, i: (0, 0))]
    args += [lam4, subln_w.reshape(A_V, 1)]
    return pl.pallas_call(
        functools.partial(_attn_body, has_ctx=has_ctx, tk=tk, lam_init=lam_init),
        grid=(b, A_HEADS, l // tq),
        in_specs=in_specs,
        out_specs=pl.BlockSpec((1, tq, A_V), lambda bi, h, i: (bi, i, h)),
        out_shape=jax.ShapeDtypeStruct((b, l, w), BF16),
        compiler_params=_cparams("parallel", "parallel", "parallel"),
        name="diff_attention",
    )(*args)


def _mlstm_mix(hf, hb, og, xc, hn_ref, sk_ref, wo_ref):
    hd = hn_ref.shape[1] // C_HEADS
    hs = hf + hb
    hn = jnp.concatenate([_rms(hs[:, h * hd:(h + 1) * hd], hn_ref[:, h * hd:(h + 1) * hd]) for h in range(C_HEADS)],
                         axis=1)
    return _dot((_sigmoid(og) * (hn + sk_ref[...] * xc)).astype(BF16), wo_ref[...])


def _even_mix(a, g, wo_ref):
    aw = a.shape[1]
    return _dot(a.astype(BF16), wo_ref[0:aw, :]) + _dot(g.astype(BF16), wo_ref[aw:, :])


def _ffn_body(*refs, seq_len, fc, final, n_acts):
    it = iter(refs)
    x_ref, xp_ref, xn_ref, nw_ref, mod_ref, wa_ref, wg_ref, cw_ref, cb_ref, wd_ref = [next(it) for _ in range(10)]
    fw_ref = next(it) if final else None
    acts = [[next(it) for _ in range(3)] for _ in range(n_acts)]
    params = [next(it) for _ in range(3 if n_acts == 4 else 1)]
    o_ref = next(it)
    tm, d = x_ref.shape
    dff = wa_ref.shape[1]
    xe = jnp.concatenate([xp_ref[...], x_ref[...], xn_ref[...]], axis=0)
    ext = [jnp.concatenate([prev[...].astype(F32)[-HALO:], main[...].astype(F32), nxt[...].astype(F32)[:HALO]], axis=0)
           for main, prev, nxt in acts]
    xe = xe + mod_ref[0, 2:3, :] * (_mlstm_mix(*ext, *params) if n_acts == 4 else _even_mix(*ext, *params))
    x = xe[HALO:HALO + tm]
    he = _modulated(xe, nw_ref[...], mod_ref[0, 3:4, :], mod_ref[0, 4:5, :]).astype(BF16)
    hc = he[HALO:HALO + tm]
    row0 = pl.program_id(0) * tm
    acc = jnp.zeros((tm, d), F32)
    for c in range(dff // fc):
        cs = slice(c * fc, (c + 1) * fc)
        ae = _dot(he, wa_ref[:, cs])
        g = _dot(hc, wg_ref[:, cs])
        conv = _conv3(ae, tm, seq_len, row0, cw_ref.at[:, cs], cb_ref.at[:, cs])
        acc = acc + _dot((_gelu(conv) * g).astype(BF16), wd_ref[cs, :])
    out = x + mod_ref[0, 5:6, :] * acc
    if final:
        out = _rms(out, fw_ref[...])
    o_ref[...] = out


def _halo_specs(tm, m, width, rows=HALO):
    nb = tm // rows
    last = m // rows - 1
    return [pl.BlockSpec((rows, width), lambda i: (jnp.maximum(i * nb - 1, 0), 0)),
            pl.BlockSpec((rows, width), lambda i: (jnp.minimum((i + 1) * nb, last), 0))]


def _conv_ffn(x, nw, mod, wa, wg, cw, cb, wd, fw, premix, rows_per_mod, seq_len, tm, fc):
    m, d = x.shape
    dff = wa.shape[1]
    final = fw is not None
    const = lambda i: (0, 0)
    in_specs = ([pl.BlockSpec((tm, d), lambda i: (i, 0))] + _halo_specs(tm, m, d)
                + [pl.BlockSpec((1, d), const),
                   pl.BlockSpec((1, N_MOD, d), lambda i: ((i * tm) // rows_per_mod, 0, 0)),
                   pl.BlockSpec((d, dff), const),
                   pl.BlockSpec((d, dff), const),
                   pl.BlockSpec((3, dff), const),
                   pl.BlockSpec((1, dff), const),
                   pl.BlockSpec((dff, d), const)])
    args = [x, x, x, nw.reshape(1, d), mod, wa, wg, cw, cb.reshape(1, dff), wd]
    if final:
        in_specs.append(pl.BlockSpec((1, d), const))
        args.append(fw.reshape(1, d))
    acts, params = premix
    for act in acts:
        in_specs += [pl.BlockSpec((tm, act.shape[1]), lambda i: (i, 0))] + _halo_specs(tm, m, act.shape[1], 2 * HALO)
        args += [act, act, act]
    for prm in params:
        prm = prm.reshape(1, -1) if prm.ndim == 1 else prm
        in_specs.append(pl.BlockSpec(prm.shape, const))
        args.append(prm)
    return pl.pallas_call(
        functools.partial(_ffn_body, seq_len=seq_len, fc=fc, final=final, n_acts=len(acts)),
        grid=(m // tm,),
        in_specs=in_specs,
        out_specs=pl.BlockSpec((tm, d), lambda i: (i, 0)),
        out_shape=jax.ShapeDtypeStruct((m, d), F32),
        compiler_params=_cparams("parallel"),
        name="conv_ffn",
    )(*args)


def _in_proj1_body(x_ref, xp_ref, xn_ref, nw_ref, mod_ref, wm_ref, wr_ref, cw_ref, cb_ref, wq_ref, wk_ref, wv_ref,
                   q_ref, kt_ref, v_ref, xc_ref, og_ref, gt_ref, *, seq_len):
    tm = x_ref.shape[0]
    ci = wm_ref.shape[1]
    hd = ci // C_HEADS
    xe = jnp.concatenate([xp_ref[...], x_ref[...], xn_ref[...]], axis=0)
    he = _modulated(xe, nw_ref[...], mod_ref[0, 0:1, :], mod_ref[0, 1:2, :]).astype(BF16)
    xme = _dot(he, wm_ref[...])
    rest = _dot(he[HALO:HALO + tm], wr_ref[...])
    og_ref[...] = rest[:, 0:ci].astype(BF16)
    gt_ref[...] = rest[:, ci:]
    conv = _conv3(xme, tm, seq_len, pl.program_id(0) * tm, cw_ref, cb_ref)
    xc = conv * _sigmoid(conv)
    xcb = xc.astype(BF16)
    xc_ref[...] = xcb
    xmb = xme[HALO:HALO + tm].astype(BF16)
    for h in range(C_HEADS):
        hs = slice(h * hd, (h + 1) * hd)
        q_ref[:, hs] = _dot(xcb[:, hs], wq_ref[h]).astype(BF16)
        k = _dot(xcb[:, hs], wk_ref[h]) * hd ** -0.5
        for r in range(tm // SCAN_T):
            kt_ref[r, hs, :] = k[r * SCAN_T:(r + 1) * SCAN_T].T.astype(BF16)
        v_ref[:, hs] = _dot(xmb[:, hs], wv_ref[h]).astype(BF16)


def _in_proj1(x, nw, mod, wm, wr, cw, cb, wq, wk, wv, rows_per_mod, seq_len, tm):
    m, d = x.shape
    ci = wm.shape[1]
    ng = wr.shape[1] - ci
    const = lambda i: (0, 0)
    const3 = lambda i: (0, 0, 0)
    row = pl.BlockSpec((tm, ci), lambda i: (i, 0))
    return pl.pallas_call(
        functools.partial(_in_proj1_body, seq_len=seq_len),
        grid=(m // tm,),
        in_specs=[pl.BlockSpec((tm, d), lambda i: (i, 0))] + _halo_specs(tm, m, d)
        + [pl.BlockSpec((1, d), const),
           pl.BlockSpec((1, N_MOD, d), lambda i: ((i * tm) // rows_per_mod, 0, 0)),
           pl.BlockSpec(wm.shape, const), pl.BlockSpec(wr.shape, const),
           pl.BlockSpec((3, ci), const), pl.BlockSpec((1, ci), const),
           pl.BlockSpec(wq.shape, const3), pl.BlockSpec(wk.shape, const3), pl.BlockSpec(wv.shape, const3)],
        out_specs=[row, pl.BlockSpec((tm // SCAN_T, ci, SCAN_T), lambda i: (i, 0, 0)), row, row, row,
                   pl.BlockSpec((tm, ng), lambda i: (i, 0))],
        out_shape=[jax.ShapeDtypeStruct((m, ci), BF16), jax.ShapeDtypeStruct((m // SCAN_T, ci, SCAN_T), BF16)]
        + [jax.ShapeDtypeStruct((m, ci), BF16)] * 3 + [jax.ShapeDtypeStruct((m, ng), F32)],
        compiler_params=_cparams("parallel"),
        name="in_proj_odd",
    )(x, x, x, nw.reshape(1, d), mod, wm, wr, cw, cb.reshape(1, ci), wq, wk, wv)


def _split3(x):
    hi = x.astype(BF16)
    r1 = x - hi.astype(F32)
    mid = r1.astype(BF16)
    lo = (r1 - mid.astype(F32)).astype(BF16)
    return hi, mid, lo


def _mlstm_scan_body(*refs, zero_init, cpb):
    it = iter(refs)
    fwd = [next(it) for _ in range(4)]
    bwd = [next(it) for _ in range(4)]
    bg_ref = next(it)
    init = None if zero_init else [next(it) for _ in range(3)]
    hf_ref, hb_ref, c_ref, n_ref, m_ref, cx_ref = [next(it) for _ in range(6)]
    t = SCAN_T
    hd = c_ref.shape[-1]
    nw = cx_ref.shape[-1] - hd
    wide = lambda a: jnp.concatenate([a] * (hd // nw), axis=1)
    j = pl.program_id(1)

    @pl.when(j == 0)
    def _():
        for dr in range(2):
            for h in range(C_HEADS):
                if zero_init:
                    cx_ref[dr, h] = jnp.zeros(cx_ref.shape[2:], F32)
                else:
                    cx_ref[dr, h, :, 0:hd] = init[0][0, dr, h]
                    cx_ref[dr, h, :, hd:] = jnp.broadcast_to(init[1][0, dr, h], (nw, hd)).T
        m_ref[...] = jnp.zeros_like(m_ref) if zero_init else init[2][...]

    rr = lax.broadcasted_iota(jnp.int32, (t, t), 0)
    cc = lax.broadcasted_iota(jnp.int32, (t, t), 1)
    ones = jnp.ones((t, nw), BF16)

    srcs = ((fwd, hf_ref), (bwd, hb_ref))
    chunk_of = lambda ci, dr: ci if dr == 0 else cpb - 1 - ci

    m_cur = {(dr, h): m_ref[0, dr, h][:, 0:1] for dr in range(2) for h in range(C_HEADS)}
    pre = {}
    for ci in range(cpb):
        for dr in range(2):
            g_ref = srcs[dr][0][3]
            ch = chunk_of(ci, dr)
            mask = (rr >= cc) if dr == 0 else (rr <= cc)
            tri = jnp.where(mask, 1.0, 0.0).astype(BF16)
            gates = g_ref[0, ch * t:(ch + 1) * t, :] + bg_ref[...]
            hi, mid, lo = _split3(_log_sigmoid(gates))
            bsum = _dot(tri, hi) + _dot(tri, mid) + _dot(tri, lo)
            bsum_t = bsum.T
            gates_t = gates.T
            for h in range(C_HEADS):
                col_i = 2 * dr * C_HEADS + h
                col_f = (2 * dr + 1) * C_HEADS + h
                b_col = jnp.broadcast_to(bsum[:, col_f:col_f + 1], (t, t))
                b_row = bsum_t[col_f:col_f + 1, :]
                i_row = gates_t[col_i:col_i + 1, :]
                m_prev = m_cur[dr, h]
                a_inter = b_col + m_prev
                dmat = jnp.where(mask, b_col - b_row + i_row, -jnp.inf)
                m_t = jnp.maximum(a_inter, jnp.max(dmat, axis=1, keepdims=True))
                b_end = b_row[:, t - 1:t] if dr == 0 else b_row[:, 0:1]
                g_row = b_end - b_row + i_row
                m_new = jnp.maximum(b_end + m_prev, jnp.max(g_row, axis=1, keepdims=True))
                m_cur[dr, h] = m_new
                pre[ci, dr, h] = (jnp.where(mask, jnp.exp(dmat - m_t), 0.0),
                                  jnp.exp(a_inter - m_t)[:, 0:nw],
                                  jnp.exp(-m_t)[:, 0:nw],
                                  jnp.exp(b_end + m_prev - m_new),
                                  jnp.exp(g_row - m_new))

    for ci in range(cpb):
        for dr in range(2):
            (q_ref, kt_ref, v_ref, _), h_out = srcs[dr]
            ch = chunk_of(ci, dr)
            rows = slice(ch * t, (ch + 1) * t)
            for h in range(C_HEADS):
                decay_w, w_inter, floor, decay, w_row = pre[ci, dr, h]
                hs = slice(h * hd, (h + 1) * hd)
                cx = cx_ref[dr, h]
                qc = q_ref[0, rows, hs]
                v1 = jnp.concatenate([v_ref[0, rows, hs], ones], axis=1)
                ktc = kt_ref[ch, hs, :]
                s = _dot(qc, ktc) * decay_w
                intra = _dot(s.astype(BF16), v1)
                inter = _dot(qc, cx.astype(BF16))
                den = intra[:, hd:] + w_inter * inter[:, hd:]
                inv = 1.0 / jnp.maximum(jnp.abs(den), floor)
                num = intra[:, 0:hd] + wide(w_inter) * inter[:, 0:hd]
                h_out[0, rows, hs] = (num * wide(inv)).astype(h_out.dtype)
                kw_t = (ktc.astype(F32) * w_row).astype(BF16)
                cx_ref[dr, h] = decay * cx + _dot(kw_t, v1)

    for (dr, h), m_new in m_cur.items():
        m_ref[0, dr, h] = jnp.broadcast_to(m_new, (1, m_ref.shape[-1]))

    @pl.when(j == pl.num_programs(1) - 1)
    def _():
        for dr in range(2):
            for h in range(C_HEADS):
                c_ref[0, dr, h] = cx_ref[dr, h, :, 0:hd]
                n_ref[0, dr, h] = cx_ref[dr, h, :, hd:].T[0:1, :]


def _mlstm_scan(q, kt, v, gates, bg, init, lb):
    b, l, ci = q.shape
    hd = ci // C_HEADS
    nb = l // lb
    cpb = lb // SCAN_T
    zero_init = init is None
    in_specs, args = [], []
    for blk in (lambda j: j, lambda j: nb - 1 - j):
        row = pl.BlockSpec((1, lb, ci), lambda bi, j, blk=blk: (bi, blk(j), 0))
        in_specs += [row,
                     pl.BlockSpec((cpb, ci, SCAN_T), lambda bi, j, blk=blk: (bi * nb + blk(j), 0, 0)),
                     row,
                     pl.BlockSpec((1, lb, gates.shape[2]), lambda bi, j, blk=blk: (bi, blk(j), 0))]
        args += [q, kt, v, gates]
    in_specs.append(pl.BlockSpec((1, 128), lambda bi, j: (0, 0)))
    args.append(bg)
    st = lambda *tail: pl.BlockSpec((1, 2, C_HEADS) + tail, lambda bi, j: (bi, 0, 0, 0, 0))
    states = [st(hd, hd), st(1, hd), st(1, 128)]
    if not zero_init:
        in_specs += states
        args += list(init)
    return pl.pallas_call(
        functools.partial(_mlstm_scan_body, zero_init=zero_init, cpb=cpb),
        grid=(b, nb),
        in_specs=in_specs,
        out_specs=[pl.BlockSpec((1, lb, ci), lambda bi, j: (bi, j, 0)),
                   pl.BlockSpec((1, lb, ci), lambda bi, j: (bi, nb - 1 - j, 0))] + states,
        out_shape=[jax.ShapeDtypeStruct((b, l, ci), BF16),
                   jax.ShapeDtypeStruct((b, l, ci), BF16),
                   jax.ShapeDtypeStruct((b, 2, C_HEADS, hd, hd), F32),
                   jax.ShapeDtypeStruct((b, 2, C_HEADS, 1, hd), F32),
                   jax.ShapeDtypeStruct((b, 2, C_HEADS, 1, 128), F32)],
        scratch_shapes=[pltpu.VMEM((2, C_HEADS, hd, hd + CHUNK), F32)],
        compiler_params=_cparams("parallel", "arbitrary"),
        name="mlstm_scan",
    )(*args)


def _rope_tables(length):
    n_rows = length // GRID_W
    rows = jnp.repeat(jnp.arange(n_rows, dtype=F32), GRID_W)
    cols = jnp.tile(jnp.arange(GRID_W, dtype=F32), n_rows)
    inv = ROPE_BASE ** (-jnp.arange(N_FREQ, dtype=F32) / N_FREQ)
    ang = jnp.stack([rows[:, None] * inv, cols[:, None] * inv], axis=1)
    cos, sin = jnp.cos(ang), jnp.sin(ang)
    zeros = jnp.zeros_like(sin)
    lanes = lambda first, second: jnp.tile(
        jnp.stack([first, second], axis=2).reshape(length, A_QK), (1, A_V // A_QK))
    return lanes(cos, cos), lanes(-sin, zeros), lanes(zeros, sin)


def _pick_tile(n, want):
    t = min(n, want)
    while n % t:
        t //= 2
    return t


def _run_group(x3, mods, ctx, init, p, tiles):
    b, l, d = x3.shape
    m = b * l
    rows_per_mod = l if mods.shape[1] > 1 else m
    tm = _pick_tile(l if mods.shape[1] > 1 else m, tiles["tm"])
    x = x3.reshape(m, d)
    extras = {}
    depth = mods.shape[0]
    for layer in range(depth):
        mod = mods[layer]
        if layer % 2 == 0:
            e = layer // 2
            lam_init = 0.8 - 0.6 * math.exp(-0.3 * layer)
            rope = _rope_tables(l) if ctx is not None else None
            outs = _in_proj0(x, p["norm1_w"][layer], mod, p["w_in0"][e], p["gate_norm_w"][e], p["w_spatial"][e],
                             p["b_spatial"][e].T, rope, ctx is None, rows_per_mod, l, tm)
            qt, k, vt, g = outs[:4]
            ctx_e = _ctx_prep(ctx[0][:, e].reshape(b, -1, A_WIDTH), ctx[1][:, e].reshape(b, -1, A_WIDTH)) \
                if ctx is not None else None
            lam4 = jnp.stack([p["lam_q1"][e], p["lam_k1"][e], p["lam_q2"][e], p["lam_k2"][e]])
            a = _attention(qt, k.reshape(b, l, A_WIDTH), vt, ctx_e, lam4, p["subln_w"][e],
                           lam_init, _pick_tile(l, tiles["tq"]), tiles["tk"])
            premix = ((a.reshape(m, A_WIDTH), g), (p["w_out0"][e],))
            for name, kv in zip("kv", outs[4:]):
                extras.setdefault(name, []).append(kv.reshape(b, l, A_HEADS, A_V))
        else:
            o = layer // 2
            ci = p["w_out1"].shape[1]
            q, kt, v, xc, og, gates = _in_proj1(x, p["norm1_w"][layer], mod, p["w_in1_m"][o], p["w_in1_r"][o],
                                                p["mconv_w"][o], p["mconv_b"][o], p["w_q"][o], p["w_k"][o],
                                                p["w_v"][o], rows_per_mod, l, tm)
            init_o = None if init is None else tuple(s[:, o] for s in init)
            hf, hb, c_fin, n_fin, m_fin = _mlstm_scan(q.reshape(b, l, ci), kt, v.reshape(b, l, ci),
                                                      gates.reshape(b, l, -1), p["b_gates"][o], init_o,
                                                      _pick_tile(l, tiles["lb"]))
            premix = ((hf.reshape(m, ci), hb.reshape(m, ci), og, xc),
                      (p["head_norm_w"][o], p["skip_w"][o], p["w_out1"][o]))
            extras.setdefault("C", []).append(c_fin)
            extras.setdefault("n", []).append(n_fin[:, :, :, 0, :])
            extras.setdefault("m", []).append(m_fin[:, :, :, 0, 0])
        fw = p["final_norm_w"] if layer == depth - 1 else None
        x = _conv_ffn(x, p["norm2_w"][layer], mod, p["w_up_a"][layer], p["w_up_g"][layer], p["fconv_w"][layer],
                      p["fconv_b"][layer], p["w_down"][layer], fw, premix, rows_per_mod, l, tm, tiles["fc"])
    return x.reshape(b, l, d), extras


def kernel(x_prompt, x_sample, cache_k, cache_v, state_C, state_n, state_m, c, c_ctx, w_mod, b_mod, norm1_w, norm2_w,
           w_in0, lam_q1, lam_k1, lam_q2, lam_k2, subln_w, gate_norm_w, w_spatial, b_spatial, w_out0, w_in1,
           b_gates, mconv_w, mconv_b, w_q, w_k, w_v, head_norm_w, skip_w, w_out1, w_up, fconv_w, fconv_b, w_down,
           final_norm_w):
    depth, d, _ = w_mod.shape
    dec_b = x_sample.shape[0]
    dff = w_down.shape[1]
    ci = w_out1.shape[1]

    n_cond = -(-(dec_b + 1) // 8) * 8
    cond = jnp.zeros((n_cond, d), F32).at[:dec_b].set(c).at[dec_b].set(c_ctx)
    mods = _modulation(cond, w_mod, b_mod).reshape(depth, n_cond, N_MOD, d)

    n_in1 = w_in1.shape[2]
    pad1 = -(-n_in1 // 128) * 128 - n_in1
    p = dict(
        norm1_w=norm1_w, norm2_w=norm2_w, lam_q1=lam_q1, lam_k1=lam_k1, lam_q2=lam_q2, lam_k2=lam_k2,
        subln_w=subln_w, gate_norm_w=gate_norm_w, b_spatial=b_spatial, mconv_w=mconv_w, mconv_b=mconv_b,
        head_norm_w=head_norm_w, skip_w=skip_w, fconv_w=fconv_w, fconv_b=fconv_b, final_norm_w=final_norm_w,
        w_in0=w_in0.astype(BF16), w_spatial=w_spatial.astype(BF16), w_out0=w_out0.astype(BF16),
        w_in1_m=w_in1[:, :, :ci].astype(BF16),
        w_in1_r=jnp.pad(w_in1[:, :, ci:], ((0, 0), (0, 0), (0, pad1))).astype(BF16),
        b_gates=jnp.pad(b_gates, ((0, 0), (0, 128 - b_gates.shape[1])))[:, None, :],
        w_q=w_q.astype(BF16), w_k=w_k.astype(BF16), w_v=w_v.astype(BF16), w_out1=w_out1.astype(BF16),
        w_up_a=w_up[:, :, :dff].astype(BF16), w_up_g=w_up[:, :, dff:].astype(BF16), w_down=w_down.astype(BF16),
    )
    tiles = dict(tm=512, tq=1024, tk=512, fc=2816, lb=1024)

    y_prompt, ex = _run_group(x_prompt, mods[:, dec_b:dec_b + 1], None, None, p, tiles)
    init = (state_C.astype(F32),
            state_n.astype(F32)[:, :, :, :, None, :],
            jnp.broadcast_to(state_m.astype(F32)[:, :, :, :, None, None], state_m.shape + (1, 128)))
    y_sample, _ = _run_group(x_sample, mods[:, :dec_b], (cache_k, cache_v), init, p, tiles)

    return (y_prompt, y_sample, jnp.stack(ex["k"], axis=1), jnp.stack(ex["v"], axis=1),
            jnp.stack(ex["C"], axis=1), jnp.stack(ex["n"], axis=1), jnp.stack(ex["m"], axis=1))
```

```python
import functools
import math

import jax
import jax.numpy as jnp
from jax import lax
from jax.experimental import pallas as pl
from jax.experimental.pallas import tpu as pltpu

F32 = jnp.float32
BF16 = jnp.bfloat16

EPS = 1e-6
ROPE_BASE = 10000.0
GRID_W = 64
N_MOD = 6
A_HEADS = 4
A_QK = 64
A_V = 2 * A_QK
A_WIDTH = A_HEADS * A_V
N_FREQ = A_QK // 4
B_GROUPS = 4
CHUNK = 128
SCAN_T = 256
C_HEADS = 4
LOG2E = 1.4426950408889634

HALO = 8
VMEM_LIMIT = 56 * 1024 * 1024
NEG_BIG = -1e30


def _cparams(*sem):
    return pltpu.CompilerParams(dimension_semantics=sem, vmem_limit_bytes=VMEM_LIMIT)


def _gelu(x):
    return 0.5 * x * (1.0 + jnp.tanh(math.sqrt(2.0 / math.pi) * (x + 0.044715 * (x * x * x))))


def _sigmoid(x):
    return 1.0 / (1.0 + jnp.exp(-x))


def _log_sigmoid(x):
    return jnp.minimum(x, 0.0) - jnp.log1p(jnp.exp(-jnp.abs(x)))


def _rms(x, w):
    return x * lax.rsqrt(jnp.mean(x * x, axis=-1, keepdims=True) + EPS) * w


def _modulated(x, nw, shift, scale):
    return _rms(x, nw) * (1.0 + scale) + shift


def _dot(a, b):
    return jnp.dot(a, b, preferred_element_type=F32)


def _conv3(xe, tm, seq_len, row0, w_ref, b_ref):
    n = xe.shape[0]
    prev = pltpu.roll(xe, 1, 0)[HALO:HALO + tm]
    nxt = pltpu.roll(xe, n - 1, 0)[HALO:HALO + tm]
    cur = xe[HALO:HALO + tm]
    pos = (row0 + lax.broadcasted_iota(jnp.int32, (tm, 1), 0)) % seq_len
    prev = jnp.where(pos == 0, 0.0, prev)
    nxt = jnp.where(pos == seq_len - 1, 0.0, nxt)
    return prev * w_ref[0:1, :] + cur * w_ref[1:2, :] + nxt * w_ref[2:3, :] + b_ref[...]


def _mod_body(c_ref, w_ref, b_ref, o_ref):
    c = c_ref[...]
    s = (c * _sigmoid(c)).astype(BF16)
    o_ref[0] = _dot(s, w_ref[0].astype(BF16)) + b_ref[0]


def _modulation(cond, w_mod, b_mod):
    depth, d, n = w_mod.shape
    r = cond.shape[0]
    tn = n // 4
    return pl.pallas_call(
        _mod_body,
        grid=(depth, n // tn),
        in_specs=[pl.BlockSpec((r, d), lambda l, j: (0, 0)),
                  pl.BlockSpec((1, d, tn), lambda l, j: (l, 0, j)),
                  pl.BlockSpec((1, 1, tn), lambda l, j: (l, 0, j))],
        out_specs=pl.BlockSpec((1, r, tn), lambda l, j: (l, 0, j)),
        out_shape=jax.ShapeDtypeStruct((depth, r, n), F32),
        compiler_params=_cparams("parallel", "parallel"),
        name="modulation",
    )(cond, w_mod, b_mod.reshape(depth, 1, n))


def _transposed_chunks(x, o_ref):
    for r in range(x.shape[0] // CHUNK):
        o_ref[r] = x[r * CHUNK:(r + 1) * CHUNK].T.astype(o_ref.dtype)


def _chunk_gmlp(gu, gv, gnw_ref, ws_ref, bs_ref):
    tm, bw = gu.shape
    ch = bw // B_GROUPS
    nch = tm // CHUNK
    vn = _rms(_gelu(gv), gnw_ref[...]).astype(BF16)
    parts = []
    for g in range(B_GROUPS):
        rhs = jnp.concatenate([vn[r * CHUNK:(r + 1) * CHUNK, g * ch:(g + 1) * ch] for r in range(nch)], axis=1)
        parts.append(_dot(ws_ref[g], rhs) + bs_ref[:, g:g + 1])
    s = jnp.concatenate(
        [jnp.concatenate([parts[g][:, r * ch:(r + 1) * ch] for g in range(B_GROUPS)], axis=1) for r in range(nch)],
        axis=0)
    return _gelu(gu) * s


def _in_proj0_body(*refs, rope, emit_kv):
    it = iter(refs)
    x_ref, nw_ref, mod_ref, w_ref, gnw_ref, ws_ref, bs_ref = [next(it) for _ in range(7)]
    tabs = [next(it) for _ in range(3)] if rope else None
    qt_ref, k_ref, vt_ref, g_ref = [next(it) for _ in range(4)]
    h = _modulated(x_ref[...], nw_ref[...], mod_ref[0, 0:1, :], mod_ref[0, 1:2, :])
    z = _dot(h.astype(BF16), w_ref[...])
    w = k_ref.shape[1]
    q, k, v = z[:, 0:w], z[:, w:2 * w], z[:, 2 * w:3 * w]
    if emit_kv:
        next(it)[...] = k
        next(it)[...] = v
    if rope:
        reps = w // tabs[0].shape[1]
        cos, sa, sb = [jnp.concatenate([tab[...]] * reps, axis=1) for tab in tabs]

        def rot(x):
            return x * cos + pltpu.roll(x, w - N_FREQ, 1) * sa + pltpu.roll(x, N_FREQ, 1) * sb

        q, k = rot(q), rot(k)
    _transposed_chunks(q * (A_QK ** -0.5 * LOG2E), qt_ref)
    k_ref[...] = k.astype(BF16)
    _transposed_chunks(v, vt_ref)
    bw = g_ref.shape[1]
    g_ref[...] = _chunk_gmlp(z[:, 3 * w:3 * w + bw], z[:, 3 * w + bw:], gnw_ref, ws_ref, bs_ref).astype(g_ref.dtype)


def _in_proj0(x, nw, mod, w_in, gnw, ws, bs_t, rope_tabs, emit_kv, rows_per_mod, seq_len, tm):
    m, d = x.shape
    n = w_in.shape[1]
    w = A_WIDTH
    bw = gnw.shape[0]
    rope = rope_tabs is not None
    row = lambda width: pl.BlockSpec((tm, width), lambda i: (i, 0))
    in_specs = [row(d),
                pl.BlockSpec((1, d), lambda i: (0, 0)),
                pl.BlockSpec((1, N_MOD, d), lambda i: ((i * tm) // rows_per_mod, 0, 0)),
                pl.BlockSpec((d, n), lambda i: (0, 0)),
                pl.BlockSpec((1, bw), lambda i: (0, 0)),
                pl.BlockSpec(ws.shape, lambda i: (0, 0, 0)),
                pl.BlockSpec(bs_t.shape, lambda i: (0, 0))]
    args = [x, nw.reshape(1, d), mod, w_in, gnw.reshape(1, bw), ws, bs_t]
    if rope:
        blocks = seq_len // tm
        in_specs += [pl.BlockSpec((tm, rope_tabs[0].shape[1]), lambda i: (i % blocks, 0))] * 3
        args += list(rope_tabs)
    chunks = pl.BlockSpec((tm // CHUNK, w, CHUNK), lambda i: (i, 0, 0))
    out_specs = [chunks, row(w), chunks, row(bw)]
    out_shape = [jax.ShapeDtypeStruct((m // CHUNK, w, CHUNK), BF16), jax.ShapeDtypeStruct((m, w), BF16),
                 jax.ShapeDtypeStruct((m // CHUNK, w, CHUNK), BF16), jax.ShapeDtypeStruct((m, bw), BF16)]
    if emit_kv:
        out_specs += [row(w), row(w)]
        out_shape += [jax.ShapeDtypeStruct((m, w), F32)] * 2
    return pl.pallas_call(
        functools.partial(_in_proj0_body, rope=rope, emit_kv=emit_kv),
        grid=(m // tm,),
        in_specs=in_specs,
        out_specs=out_specs,
        out_shape=out_shape,
        compiler_params=_cparams("parallel"),
        name="in_proj_even",
    )(*args)


def _ctx_prep_body(k_ref, v_ref, ko_ref, vt_ref):
    ko_ref[0] = k_ref[0].astype(BF16)
    _transposed_chunks(v_ref[0], vt_ref)


def _ctx_prep(ctx_k, ctx_v):
    b, p, w = ctx_k.shape
    return pl.pallas_call(
        _ctx_prep_body,
        grid=(b,),
        in_specs=[pl.BlockSpec((1, p, w), lambda bi: (bi, 0, 0))] * 2,
        out_specs=[pl.BlockSpec((1, p, w), lambda bi: (bi, 0, 0)),
                   pl.BlockSpec((p // CHUNK, w, CHUNK), lambda bi: (bi, 0, 0))],
        out_shape=[jax.ShapeDtypeStruct((b, p, w), BF16), jax.ShapeDtypeStruct((b * p // CHUNK, w, CHUNK), BF16)],
        compiler_params=_cparams("parallel"),
        name="ctx_prep",
    )(ctx_k, ctx_v)


def _joined(ref, first, count):
    return jnp.concatenate([ref[first + r] for r in range(count)], axis=1)


def _attn_body(*refs, has_ctx, tk, lam_init):
    if has_ctx:
        qt_ref, k_ref, vt_ref, ck_ref, cvt_ref, lam_ref, sw_ref, o_ref = refs
    else:
        qt_ref, k_ref, vt_ref, lam_ref, sw_ref, o_ref = refs
    qt = _joined(qt_ref, 0, qt_ref.shape[0])
    tq = qt.shape[1]
    sub = lax.broadcasted_iota(jnp.int32, qt.shape, 0)
    zero = jnp.zeros_like(qt)
    qq = jnp.concatenate([jnp.where(sub < A_QK, qt, zero), jnp.where(sub >= A_QK, qt, zero)], axis=1)

    m = jnp.full((1, 2 * tq), NEG_BIG, F32)
    l = jnp.zeros((1, 2 * tq), F32)
    acc = jnp.zeros((A_V, 2 * tq), F32)

    chunks = []
    for k_src, vt_src in ([(ck_ref, cvt_ref)] if has_ctx else []) + [(k_ref, vt_ref)]:
        lk = k_src.shape[1]
        ck = _pick_tile(lk, tk)
        chunks += [(k_src, vt_src, c, ck) for c in range(lk // ck)]
    scores = lambda k_src, vt_src, c, ck: _dot(k_src[0, c * ck:(c + 1) * ck, :], qq)
    s_next = scores(*chunks[0])
    for i, (k_src, vt_src, c, ck) in enumerate(chunks):
        s = s_next
        if i + 1 < len(chunks):
            s_next = scores(*chunks[i + 1])
        m_new = jnp.maximum(m, jnp.max(s, axis=0, keepdims=True))
        alpha = jnp.exp2(m - m_new)
        p = jnp.exp2(s - m_new)
        l = alpha * l + jnp.sum(p, axis=0, keepdims=True)
        acc = alpha * acc + _dot(_joined(vt_src, c * (ck // CHUNK), ck // CHUNK), p.astype(BF16))
        m = m_new

    o = acc / l
    lam = (jnp.exp(jnp.sum(lam_ref[0:1, :] * lam_ref[1:2, :], axis=-1, keepdims=True))
           - jnp.exp(jnp.sum(lam_ref[2:3, :] * lam_ref[3:4, :], axis=-1, keepdims=True)) + lam_init)
    a = o[:, 0:tq] - lam * o[:, tq:]
    an = a * lax.rsqrt(jnp.mean(a * a, axis=0, keepdims=True) + EPS) * sw_ref[...] * (1.0 - lam_init)
    o_ref[0] = an.T.astype(o_ref.dtype)


def _attention(qt, k, vt, ctx, lam4, subln_w, lam_init, tq, tk):
    b, l, w = k.shape
    has_ctx = ctx is not None
    k_spec = lambda n: pl.BlockSpec((1, n, A_V), lambda bi, h, i: (bi, 0, h))
    vt_spec = lambda n: pl.BlockSpec((n // CHUNK, A_V, CHUNK), lambda bi, h, i: (bi, h, 0))
    nq = l // tq
    in_specs = [pl.BlockSpec((tq // CHUNK, A_V, CHUNK), lambda bi, h, i: (bi * nq + i, h, 0)), k_spec(l), vt_spec(l)]
    args = [qt, k, vt]
    if has_ctx:
        p = ctx[0].shape[1]
        in_specs += [k_spec(p), vt_spec(p)]
        args += list(ctx)
    in_specs += [pl.BlockSpec((4, A_QK), lambda bi, h, i: (0, 0)),
                 pl.BlockSpec((A_V, 1), lambda bi, h, i: (0, 0))]
    args += [lam4, subln_w.reshape(A_V, 1)]
    return pl.pallas_call(
        functools.partial(_attn_body, has_ctx=has_ctx, tk=tk, lam_init=lam_init),
        grid=(b, A_HEADS, l // tq),
        in_specs=in_specs,
        out_specs=pl.BlockSpec((1, tq, A_V), lambda bi, h, i: (bi, i, h)),
        out_shape=jax.ShapeDtypeStruct((b, l, w), BF16),
        compiler_params=_cparams("parallel", "parallel", "parallel"),
        name="diff_attention",
    )(*args)


def _mlstm_mix(hf, hb, og, xc, hn_ref, sk_ref, wo_ref):
    hd = hn_ref.shape[1] // C_HEADS
    hs = hf + hb
    hn = jnp.concatenate([_rms(hs[:, h * hd:(h + 1) * hd], hn_ref[:, h * hd:(h + 1) * hd]) for h in range(C_HEADS)],
                         axis=1)
    return _dot((_sigmoid(og) * (hn + sk_ref[...] * xc)).astype(BF16), wo_ref[...])


def _even_mix(a, g, wo_ref):
    aw = a.shape[1]
    return _dot(a.astype(BF16), wo_ref[0:aw, :]) + _dot(g.astype(BF16), wo_ref[aw:, :])


def _ffn_body(*refs, seq_len, fc, final, n_acts):
    it = iter(refs)
    x_ref, xp_ref, xn_ref, nw_ref, mod_ref, wa_ref, wg_ref, cw_ref, cb_ref, wd_ref = [next(it) for _ in range(10)]
    fw_ref = next(it) if final else None
    acts = [[next(it) for _ in range(3)] for _ in range(n_acts)]
    params = [next(it) for _ in range(3 if n_acts == 4 else 1)]
    o_ref = next(it)
    tm, d = x_ref.shape
    dff = wa_ref.shape[1]
    xe = jnp.concatenate([xp_ref[...], x_ref[...], xn_ref[...]], axis=0)
    ext = [jnp.concatenate([prev[...].astype(F32)[-HALO:], main[...].astype(F32), nxt[...].astype(F32)[:HALO]], axis=0)
           for main, prev, nxt in acts]
    xe = xe + mod_ref[0, 2:3, :] * (_mlstm_mix(*ext, *params) if n_acts == 4 else _even_mix(*ext, *params))
    x = xe[HALO:HALO + tm]
    he = _modulated(xe, nw_ref[...], mod_ref[0, 3:4, :], mod_ref[0, 4:5, :]).astype(BF16)
    hc = he[HALO:HALO + tm]
    row0 = pl.program_id(0) * tm
    acc = jnp.zeros((tm, d), F32)
    for c in range(dff // fc):
        cs = slice(c * fc, (c + 1) * fc)
        ae = _dot(he, wa_ref[:, cs])
        g = _dot(hc, wg_ref[:, cs])
        conv = _conv3(ae, tm, seq_len, row0, cw_ref.at[:, cs], cb_ref.at[:, cs])
        acc = acc + _dot((_gelu(conv) * g).astype(BF16), wd_ref[cs, :])
    out = x + mod_ref[0, 5:6, :] * acc
    if final:
        out = _rms(out, fw_ref[...])
    o_ref[...] = out


def _halo_specs(tm, m, width, rows=HALO):
    nb = tm // rows
    last = m // rows - 1
    return [pl.BlockSpec((rows, width), lambda i: (jnp.maximum(i * nb - 1, 0), 0)),
            pl.BlockSpec((rows, width), lambda i: (jnp.minimum((i + 1) * nb, last), 0))]


def _conv_ffn(x, nw, mod, wa, wg, cw, cb, wd, fw, premix, rows_per_mod, seq_len, tm, fc):
    m, d = x.shape
    dff = wa.shape[1]
    final = fw is not None
    const = lambda i: (0, 0)
    in_specs = ([pl.BlockSpec((tm, d), lambda i: (i, 0))] + _halo_specs(tm, m, d)
                + [pl.BlockSpec((1, d), const),
                   pl.BlockSpec((1, N_MOD, d), lambda i: ((i * tm) // rows_per_mod, 0, 0)),
                   pl.BlockSpec((d, dff), const),
                   pl.BlockSpec((d, dff), const),
                   pl.BlockSpec((3, dff), const),
                   pl.BlockSpec((1, dff), const),
                   pl.BlockSpec((dff, d), const)])
    args = [x, x, x, nw.reshape(1, d), mod, wa, wg, cw, cb.reshape(1, dff), wd]
    if final:
        in_specs.append(pl.BlockSpec((1, d), const))
        args.append(fw.reshape(1, d))
    acts, params = premix
    for act in acts:
        in_specs += [pl.BlockSpec((tm, act.shape[1]), lambda i: (i, 0))] + _halo_specs(tm, m, act.shape[1], 2 * HALO)
        args += [act, act, act]
    for prm in params:
        prm = prm.reshape(1, -1) if prm.ndim == 1 else prm
        in_specs.append(pl.BlockSpec(prm.shape, const))
        args.append(prm)
    return pl.pallas_call(
        functools.partial(_ffn_body, seq_len=seq_len, fc=fc, final=final, n_acts=len(acts)),
        grid=(m // tm,),
        in_specs=in_specs,
        out_specs=pl.BlockSpec((tm, d), lambda i: (i, 0)),
        out_shape=jax.ShapeDtypeStruct((m, d), F32),
        compiler_params=_cparams("parallel"),
        name="conv_ffn",
    )(*args)


def _in_proj1_body(x_ref, xp_ref, xn_ref, nw_ref, mod_ref, wm_ref, wr_ref, cw_ref, cb_ref, wq_ref, wk_ref, wv_ref,
                   q_ref, kt_ref, v_ref, xc_ref, og_ref, gt_ref, *, seq_len):
    tm = x_ref.shape[0]
    ci = wm_ref.shape[1]
    hd = ci // C_HEADS
    xe = jnp.concatenate([xp_ref[...], x_ref[...], xn_ref[...]], axis=0)
    he = _modulated(xe, nw_ref[...], mod_ref[0, 0:1, :], mod_ref[0, 1:2, :]).astype(BF16)
    xme = _dot(he, wm_ref[...])
    rest = _dot(he[HALO:HALO + tm], wr_ref[...])
    og_ref[...] = rest[:, 0:ci].astype(BF16)
    gt_ref[...] = rest[:, ci:]
    conv = _conv3(xme, tm, seq_len, pl.program_id(0) * tm, cw_ref, cb_ref)
    xc = conv * _sigmoid(conv)
    xcb = xc.astype(BF16)
    xc_ref[...] = xcb
    xmb = xme[HALO:HALO + tm].astype(BF16)
    for h in range(C_HEADS):
        hs = slice(h * hd, (h + 1) * hd)
        q_ref[:, hs] = _dot(xcb[:, hs], wq_ref[h]).astype(BF16)
        k = _dot(xcb[:, hs], wk_ref[h]) * hd ** -0.5
        for r in range(tm // SCAN_T):
            kt_ref[r, hs, :] = k[r * SCAN_T:(r + 1) * SCAN_T].T.astype(BF16)
        v_ref[:, hs] = _dot(xmb[:, hs], wv_ref[h]).astype(BF16)


def _in_proj1(x, nw, mod, wm, wr, cw, cb, wq, wk, wv, rows_per_mod, seq_len, tm):
    m, d = x.shape
    ci = wm.shape[1]
    ng = wr.shape[1] - ci
    const = lambda i: (0, 0)
    const3 = lambda i: (0, 0, 0)
    row = pl.BlockSpec((tm, ci), lambda i: (i, 0))
    return pl.pallas_call(
        functools.partial(_in_proj1_body, seq_len=seq_len),
        grid=(m // tm,),
        in_specs=[pl.BlockSpec((tm, d), lambda i: (i, 0))] + _halo_specs(tm, m, d)
        + [pl.BlockSpec((1, d), const),
           pl.BlockSpec((1, N_MOD, d), lambda i: ((i * tm) // rows_per_mod, 0, 0)),
           pl.BlockSpec(wm.shape, const), pl.BlockSpec(wr.shape, const),
           pl.BlockSpec((3, ci), const), pl.BlockSpec((1, ci), const),
           pl.BlockSpec(wq.shape, const3), pl.BlockSpec(wk.shape, const3), pl.BlockSpec(wv.shape, const3)],
        out_specs=[row, pl.BlockSpec((tm // SCAN_T, ci, SCAN_T), lambda i: (i, 0, 0)), row, row, row,
                   pl.BlockSpec((tm, ng), lambda i: (i, 0))],
        out_shape=[jax.ShapeDtypeStruct((m, ci), BF16), jax.ShapeDtypeStruct((m // SCAN_T, ci, SCAN_T), BF16)]
        + [jax.ShapeDtypeStruct((m, ci), BF16)] * 3 + [jax.ShapeDtypeStruct((m, ng), F32)],
        compiler_params=_cparams("parallel"),
        name="in_proj_odd",
    )(x, x, x, nw.reshape(1, d), mod, wm, wr, cw, cb.reshape(1, ci), wq, wk, wv)


def _split3(x):
    hi = x.astype(BF16)
    r1 = x - hi.astype(F32)
    mid = r1.astype(BF16)
    lo = (r1 - mid.astype(F32)).astype(BF16)
    return hi, mid, lo


def _mlstm_scan_body(*refs, zero_init, cpb):
    it = iter(refs)
    fwd = [next(it) for _ in range(4)]
    bwd = [next(it) for _ in range(4)]
    bg_ref = next(it)
    init = None if zero_init else [next(it) for _ in range(3)]
    hf_ref, hb_ref, c_ref, n_ref, m_ref, cx_ref = [next(it) for _ in range(6)]
    t = SCAN_T
    hd = c_ref.shape[-1]
    nw = cx_ref.shape[-1] - hd
    wide = lambda a: jnp.concatenate([a] * (hd // nw), axis=1)
    j = pl.program_id(1)

    @pl.when(j == 0)
    def _():
        for dr in range(2):
            for h in range(C_HEADS):
                if zero_init:
                    cx_ref[dr, h] = jnp.zeros(cx_ref.shape[2:], F32)
                else:
                    cx_ref[dr, h, :, 0:hd] = init[0][0, dr, h]
                    cx_ref[dr, h, :, hd:] = jnp.broadcast_to(init[1][0, dr, h], (nw, hd)).T
        m_ref[...] = jnp.zeros_like(m_ref) if zero_init else init[2][...]

    rr = lax.broadcasted_iota(jnp.int32, (t, t), 0)
    cc = lax.broadcasted_iota(jnp.int32, (t, t), 1)
    ones = jnp.ones((t, nw), BF16)

    srcs = ((fwd, hf_ref), (bwd, hb_ref))
    chunk_of = lambda ci, dr: ci if dr == 0 else cpb - 1 - ci

    m_cur = {(dr, h): m_ref[0, dr, h][:, 0:1] for dr in range(2) for h in range(C_HEADS)}
    pre = {}
    for ci in range(cpb):
        for dr in range(2):
            g_ref = srcs[dr][0][3]
            ch = chunk_of(ci, dr)
            mask = (rr >= cc) if dr == 0 else (rr <= cc)
            tri = jnp.where(mask, 1.0, 0.0).astype(BF16)
            gates = g_ref[0, ch * t:(ch + 1) * t, :] + bg_ref[...]
            hi, mid, lo = _split3(_log_sigmoid(gates))
            bsum = _dot(tri, hi) + _dot(tri, mid) + _dot(tri, lo)
            bsum_t = bsum.T
            gates_t = gates.T
            for h in range(C_HEADS):
                col_i = 2 * dr * C_HEADS + h
                col_f = (2 * dr + 1) * C_HEADS + h
                b_col = jnp.broadcast_to(bsum[:, col_f:col_f + 1], (t, t))
                b_row = bsum_t[col_f:col_f + 1, :]
                i_row = gates_t[col_i:col_i + 1, :]
                m_prev = m_cur[dr, h]
                a_inter = b_col + m_prev
                dmat = jnp.where(mask, b_col - b_row + i_row, -jnp.inf)
                m_t = jnp.maximum(a_inter, jnp.max(dmat, axis=1, keepdims=True))
                b_end = b_row[:, t - 1:t] if dr == 0 else b_row[:, 0:1]
                g_row = b_end - b_row + i_row
                m_new = jnp.maximum(b_end + m_prev, jnp.max(g_row, axis=1, keepdims=True))
                m_cur[dr, h] = m_new
                pre[ci, dr, h] = (jnp.where(mask, jnp.exp(dmat - m_t), 0.0),
                                  jnp.exp(a_inter - m_t)[:, 0:nw],
                                  jnp.exp(-m_t)[:, 0:nw],
                                  jnp.exp(b_end + m_prev - m_new),
                                  jnp.exp(g_row - m_new))

    for ci in range(cpb):
        for dr in range(2):
            (q_ref, kt_ref, v_ref, _), h_out = srcs[dr]
            ch = chunk_of(ci, dr)
            rows = slice(ch * t, (ch + 1) * t)
            for h in range(C_HEADS):
                decay_w, w_inter, floor, decay, w_row = pre[ci, dr, h]
                hs = slice(h * hd, (h + 1) * hd)
                cx = cx_ref[dr, h]
                qc = q_ref[0, rows, hs]
                v1 = jnp.concatenate([v_ref[0, rows, hs], ones], axis=1)
                ktc = kt_ref[ch, hs, :]
                s = _dot(qc, ktc) * decay_w
                intra = _dot(s.astype(BF16), v1)
                inter = _dot(qc, cx.astype(BF16))
                den = intra[:, hd:] + w_inter * inter[:, hd:]
                inv = 1.0 / jnp.maximum(jnp.abs(den), floor)
                num = intra[:, 0:hd] + wide(w_inter) * inter[:, 0:hd]
                h_out[0, rows, hs] = (num * wide(inv)).astype(h_out.dtype)
                kw_t = (ktc.astype(F32) * w_row).astype(BF16)
                cx_ref[dr, h] = decay * cx + _dot(kw_t, v1)

    for (dr, h), m_new in m_cur.items():
        m_ref[0, dr, h] = jnp.broadcast_to(m_new, (1, m_ref.shape[-1]))

    @pl.when(j == pl.num_programs(1) - 1)
    def _():
        for dr in range(2):
            for h in range(C_HEADS):
                c_ref[0, dr, h] = cx_ref[dr, h, :, 0:hd]
                n_ref[0, dr, h] = cx_ref[dr, h, :, hd:].T[0:1, :]


def _mlstm_scan(q, kt, v, gates, bg, init, lb):
    b, l, ci = q.shape
    hd = ci // C_HEADS
    nb = l // lb
    cpb = lb // SCAN_T
    zero_init = init is None
    in_specs, args = [], []
    for blk in (lambda j: j, lambda j: nb - 1 - j):
        row = pl.BlockSpec((1, lb, ci), lambda bi, j, blk=blk: (bi, blk(j), 0))
        in_specs += [row,
                     pl.BlockSpec((cpb, ci, SCAN_T), lambda bi, j, blk=blk: (bi * nb + blk(j), 0, 0)),
                     row,
                     pl.BlockSpec((1, lb, gates.shape[2]), lambda bi, j, blk=blk: (bi, blk(j), 0))]
        args += [q, kt, v, gates]
    in_specs.append(pl.BlockSpec((1, 128), lambda bi, j: (0, 0)))
    args.append(bg)
    st = lambda *tail: pl.BlockSpec((1, 2, C_HEADS) + tail, lambda bi, j: (bi, 0, 0, 0, 0))
    states = [st(hd, hd), st(1, hd), st(1, 128)]
    if not zero_init:
        in_specs += states
        args += list(init)
    return pl.pallas_call(
        functools.partial(_mlstm_scan_body, zero_init=zero_init, cpb=cpb),
        grid=(b, nb),
        in_specs=in_specs,
        out_specs=[pl.BlockSpec((1, lb, ci), lambda bi, j: (bi, j, 0)),
                   pl.BlockSpec((1, lb, ci), lambda bi, j: (bi, nb - 1 - j, 0))] + states,
        out_shape=[jax.ShapeDtypeStruct((b, l, ci), BF16),
                   jax.ShapeDtypeStruct((b, l, ci), BF16),
                   jax.ShapeDtypeStruct((b, 2, C_HEADS, hd, hd), F32),
                   jax.ShapeDtypeStruct((b, 2, C_HEADS, 1, hd), F32),
                   jax.ShapeDtypeStruct((b, 2, C_HEADS, 1, 128), F32)],
        scratch_shapes=[pltpu.VMEM((2, C_HEADS, hd, hd + CHUNK), F32)],
        compiler_params=_cparams("parallel", "arbitrary"),
        name="mlstm_scan",
    )(*args)


def _rope_tables(length):
    n_rows = length // GRID_W
    rows = jnp.repeat(jnp.arange(n_rows, dtype=F32), GRID_W)
    cols = jnp.tile(jnp.arange(GRID_W, dtype=F32), n_rows)
    inv = ROPE_BASE ** (-jnp.arange(N_FREQ, dtype=F32) / N_FREQ)
    ang = jnp.stack([rows[:, None] * inv, cols[:, None] * inv], axis=1)
    cos, sin = jnp.cos(ang), jnp.sin(ang)
    zeros = jnp.zeros_like(sin)
    lanes = lambda first, second: jnp.tile(
        jnp.stack([first, second], axis=2).reshape(length, A_QK), (1, A_V // A_QK))
    return lanes(cos, cos), lanes(-sin, zeros), lanes(zeros, sin)


def _pick_tile(n, want):
    t = min(n, want)
    while n % t:
        t //= 2
    return t


def _run_group(x3, mods, ctx, init, p, tiles):
    b, l, d = x3.shape
    m = b * l
    rows_per_mod = l if mods.shape[1] > 1 else m
    tm = _pick_tile(l if mods.shape[1] > 1 else m, tiles["tm"])
    x = x3.reshape(m, d)
    extras = {}
    depth = mods.shape[0]
    for layer in range(depth):
        mod = mods[layer]
        if layer % 2 == 0:
            e = layer // 2
            lam_init = 0.8 - 0.6 * math.exp(-0.3 * layer)
            rope = _rope_tables(l) if ctx is not None else None
            outs = _in_proj0(x, p["norm1_w"][layer], mod, p["w_in0"][e], p["gate_norm_w"][e], p["w_spatial"][e],
                             p["b_spatial"][e].T, rope, ctx is None, rows_per_mod, l, tm)
            qt, k, vt, g = outs[:4]
            ctx_e = _ctx_prep(ctx[0][:, e].reshape(b, -1, A_WIDTH), ctx[1][:, e].reshape(b, -1, A_WIDTH)) \
                if ctx is not None else None
            lam4 = jnp.stack([p["lam_q1"][e], p["lam_k1"][e], p["lam_q2"][e], p["lam_k2"][e]])
            a = _attention(qt, k.reshape(b, l, A_WIDTH), vt, ctx_e, lam4, p["subln_w"][e],
                           lam_init, _pick_tile(l, tiles["tq"]), tiles["tk"])
            premix = ((a.reshape(m, A_WIDTH), g), (p["w_out0"][e],))
            for name, kv in zip("kv", outs[4:]):
                extras.setdefault(name, []).append(kv.reshape(b, l, A_HEADS, A_V))
        else:
            o = layer // 2
            ci = p["w_out1"].shape[1]
            q, kt, v, xc, og, gates = _in_proj1(x, p["norm1_w"][layer], mod, p["w_in1_m"][o], p["w_in1_r"][o],
                                                p["mconv_w"][o], p["mconv_b"][o], p["w_q"][o], p["w_k"][o],
                                                p["w_v"][o], rows_per_mod, l, tm)
            init_o = None if init is None else tuple(s[:, o] for s in init)
            hf, hb, c_fin, n_fin, m_fin = _mlstm_scan(q.reshape(b, l, ci), kt, v.reshape(b, l, ci),
                                                      gates.reshape(b, l, -1), p["b_gates"][o], init_o,
                                                      _pick_tile(l, tiles["lb"]))
            premix = ((hf.reshape(m, ci), hb.reshape(m, ci), og, xc),
                      (p["head_norm_w"][o], p["skip_w"][o], p["w_out1"][o]))
            extras.setdefault("C", []).append(c_fin)
            extras.setdefault("n", []).append(n_fin[:, :, :, 0, :])
            extras.setdefault("m", []).append(m_fin[:, :, :, 0, 0])
        fw = p["final_norm_w"] if layer == depth - 1 else None
        x = _conv_ffn(x, p["norm2_w"][layer], mod, p["w_up_a"][layer], p["w_up_g"][layer], p["fconv_w"][layer],
                      p["fconv_b"][layer], p["w_down"][layer], fw, premix, rows_per_mod, l, tm, tiles["fc"])
    return x.reshape(b, l, d), extras


def kernel(x_prompt, x_sample, cache_k, cache_v, state_C, state_n, state_m, c, c_ctx, w_mod, b_mod, norm1_w, norm2_w,
           w_in0, lam_q1, lam_k1, lam_q2, lam_k2, subln_w, gate_norm_w, w_spatial, b_spatial, w_out0, w_in1,
           b_gates, mconv_w, mconv_b, w_q, w_k, w_v, head_norm_w, skip_w, w_out1, w_up, fconv_w, fconv_b, w_down,
           final_norm_w):
    depth, d, _ = w_mod.shape
    dec_b = x_sample.shape[0]
    dff = w_down.shape[1]
    ci = w_out1.shape[1]

    n_cond = -(-(dec_b + 1) // 8) * 8
    cond = jnp.zeros((n_cond, d), F32).at[:dec_b].set(c).at[dec_b].set(c_ctx)
    mods = _modulation(cond, w_mod, b_mod).reshape(depth, n_cond, N_MOD, d)

    n_in1 = w_in1.shape[2]
    pad1 = -(-n_in1 // 128) * 128 - n_in1
    p = dict(
        norm1_w=norm1_w, norm2_w=norm2_w, lam_q1=lam_q1, lam_k1=lam_k1, lam_q2=lam_q2, lam_k2=lam_k2,
        subln_w=subln_w, gate_norm_w=gate_norm_w, b_spatial=b_spatial, mconv_w=mconv_w, mconv_b=mconv_b,
        head_norm_w=head_norm_w, skip_w=skip_w, fconv_w=fconv_w, fconv_b=fconv_b, final_norm_w=final_norm_w,
        w_in0=w_in0.astype(BF16), w_spatial=w_spatial.astype(BF16), w_out0=w_out0.astype(BF16),
        w_in1_m=w_in1[:, :, :ci].astype(BF16),
        w_in1_r=jnp.pad(w_in1[:, :, ci:], ((0, 0), (0, 0), (0, pad1))).astype(BF16),
        b_gates=jnp.pad(b_gates, ((0, 0), (0, 128 - b_gates.shape[1])))[:, None, :],
        w_q=w_q.astype(BF16), w_k=w_k.astype(BF16), w_v=w_v.astype(BF16), w_out1=w_out1.astype(BF16),
        w_up_a=w_up[:, :, :dff].astype(BF16), w_up_g=w_up[:, :, dff:].astype(BF16), w_down=w_down.astype(BF16),
    )
    tiles = dict(tm=512, tq=1024, tk=512, fc=2816, lb=1024)

    y_prompt, ex = _run_group(x_prompt, mods[:, dec_b:dec_b + 1], None, None, p, tiles)
    init = (state_C.astype(F32),
            state_n.astype(F32)[:, :, :, :, None, :],
            jnp.broadcast_to(state_m.astype(F32)[:, :, :, :, None, None], state_m.shape + (1, 128)))
    y_sample, _ = _run_group(x_sample, mods[:, :dec_b], (cache_k, cache_v), init, p, tiles)

    return (y_prompt, y_sample, jnp.stack(ex["k"], axis=1), jnp.stack(ex["v"], axis=1),
            jnp.stack(ex["C"], axis=1), jnp.stack(ex["n"], axis=1), jnp.stack(ex["m"], axis=1))
```

```python
import functools
import math

import jax
import jax.numpy as jnp
from jax import lax
from jax.experimental import pallas as pl
from jax.experimental.pallas import tpu as pltpu

F32 = jnp.float32
BF16 = jnp.bfloat16

EPS = 1e-6
ROPE_BASE = 10000.0
GRID_W = 64
N_MOD = 6
A_HEADS = 4
A_QK = 64
A_V = 2 * A_QK
A_WIDTH = A_HEADS * A_V
N_FREQ = A_QK // 4
B_GROUPS = 4
CHUNK = 128
SCAN_T = 256
C_HEADS = 4
LOG2E = 1.4426950408889634

HALO = 8
VMEM_LIMIT = 56 * 1024 * 1024
NEG_BIG = -1e30


def _cparams(*sem):
    return pltpu.CompilerParams(dimension_semantics=sem, vmem_limit_bytes=VMEM_LIMIT)


def _gelu(x):
    return 0.5 * x * (1.0 + jnp.tanh(math.sqrt(2.0 / math.pi) * (x + 0.044715 * (x * x * x))))


def _sigmoid(x):
    return 1.0 / (1.0 + jnp.exp(-x))


def _log_sigmoid(x):
    return jnp.minimum(x, 0.0) - jnp.log1p(jnp.exp(-jnp.abs(x)))


def _rms(x, w):
    return x * lax.rsqrt(jnp.mean(x * x, axis=-1, keepdims=True) + EPS) * w


def _modulated(x, nw, shift, scale):
    return _rms(x, nw) * (1.0 + scale) + shift


def _dot(a, b):
    return jnp.dot(a, b, preferred_element_type=F32)


def _conv3(xe, tm, seq_len, row0, w_ref, b_ref):
    n = xe.shape[0]
    prev = pltpu.roll(xe, 1, 0)[HALO:HALO + tm]
    nxt = pltpu.roll(xe, n - 1, 0)[HALO:HALO + tm]
    cur = xe[HALO:HALO + tm]
    pos = (row0 + lax.broadcasted_iota(jnp.int32, (tm, 1), 0)) % seq_len
    prev = jnp.where(pos == 0, 0.0, prev)
    nxt = jnp.where(pos == seq_len - 1, 0.0, nxt)
    return prev * w_ref[0:1, :] + cur * w_ref[1:2, :] + nxt * w_ref[2:3, :] + b_ref[...]


def _mod_body(c_ref, w_ref, b_ref, o_ref):
    c = c_ref[...]
    s = (c * _sigmoid(c)).astype(BF16)
    o_ref[0] = _dot(s, w_ref[0].astype(BF16)) + b_ref[0]


def _modulation(cond, w_mod, b_mod):
    depth, d, n = w_mod.shape
    r = cond.shape[0]
    tn = n // 4
    return pl.pallas_call(
        _mod_body,
        grid=(depth, n // tn),
        in_specs=[pl.BlockSpec((r, d), lambda l, j: (0, 0)),
                  pl.BlockSpec((1, d, tn), lambda l, j: (l, 0, j)),
                  pl.BlockSpec((1, 1, tn), lambda l, j: (l, 0, j))],
        out_specs=pl.BlockSpec((1, r, tn), lambda l, j: (l, 0, j)),
        out_shape=jax.ShapeDtypeStruct((depth, r, n), F32),
        compiler_params=_cparams("parallel", "parallel"),
        name="modulation",
    )(cond, w_mod, b_mod.reshape(depth, 1, n))


def _transposed_chunks(x, o_ref):
    for r in range(x.shape[0] // CHUNK):
        o_ref[r] = x[r * CHUNK:(r + 1) * CHUNK].T.astype(o_ref.dtype)


def _chunk_gmlp(gu, gv, gnw_ref, ws_ref, bs_ref):
    tm, bw = gu.shape
    ch = bw // B_GROUPS
    nch = tm // CHUNK
    vn = _rms(_gelu(gv), gnw_ref[...]).astype(BF16)
    parts = []
    for g in range(B_GROUPS):
        rhs = jnp.concatenate([vn[r * CHUNK:(r + 1) * CHUNK, g * ch:(g + 1) * ch] for r in range(nch)], axis=1)
        parts.append(_dot(ws_ref[g], rhs) + bs_ref[:, g:g + 1])
    s = jnp.concatenate(
        [jnp.concatenate([parts[g][:, r * ch:(r + 1) * ch] for g in range(B_GROUPS)], axis=1) for r in range(nch)],
        axis=0)
    return _gelu(gu) * s


def _in_proj0_body(*refs, rope, emit_kv):
    it = iter(refs)
    x_ref, nw_ref, mod_ref, w_ref, gnw_ref, ws_ref, bs_ref = [next(it) for _ in range(7)]
    tabs = [next(it) for _ in range(3)] if rope else None
    qt_ref, k_ref, vt_ref, g_ref = [next(it) for _ in range(4)]
    h = _modulated(x_ref[...], nw_ref[...], mod_ref[0, 0:1, :], mod_ref[0, 1:2, :])
    z = _dot(h.astype(BF16), w_ref[...])
    w = k_ref.shape[1]
    q, k, v = z[:, 0:w], z[:, w:2 * w], z[:, 2 * w:3 * w]
    if emit_kv:
        next(it)[...] = k
        next(it)[...] = v
    if rope:
        reps = w // tabs[0].shape[1]
        cos, sa, sb = [jnp.concatenate([tab[...]] * reps, axis=1) for tab in tabs]

        def rot(x):
            return x * cos + pltpu.roll(x, w - N_FREQ, 1) * sa + pltpu.roll(x, N_FREQ, 1) * sb

        q, k = rot(q), rot(k)
    _transposed_chunks(q * (A_QK ** -0.5 * LOG2E), qt_ref)
    k_ref[...] = k.astype(BF16)
    _transposed_chunks(v, vt_ref)
    bw = g_ref.shape[1]
    g_ref[...] = _chunk_gmlp(z[:, 3 * w:3 * w + bw], z[:, 3 * w + bw:], gnw_ref, ws_ref, bs_ref).astype(g_ref.dtype)


def _in_proj0(x, nw, mod, w_in, gnw, ws, bs_t, rope_tabs, emit_kv, rows_per_mod, seq_len, tm):
    m, d = x.shape
    n = w_in.shape[1]
    w = A_WIDTH
    bw = gnw.shape[0]
    rope = rope_tabs is not None
    row = lambda width: pl.BlockSpec((tm, width), lambda i: (i, 0))
    in_specs = [row(d),
                pl.BlockSpec((1, d), lambda i: (0, 0)),
                pl.BlockSpec((1, N_MOD, d), lambda i: ((i * tm) // rows_per_mod, 0, 0)),
                pl.BlockSpec((d, n), lambda i: (0, 0)),
                pl.BlockSpec((1, bw), lambda i: (0, 0)),
                pl.BlockSpec(ws.shape, lambda i: (0, 0, 0)),
                pl.BlockSpec(bs_t.shape, lambda i: (0, 0))]
    args = [x, nw.reshape(1, d), mod, w_in, gnw.reshape(1, bw), ws, bs_t]
    if rope:
        blocks = seq_len // tm
        in_specs += [pl.BlockSpec((tm, rope_tabs[0].shape[1]), lambda i: (i % blocks, 0))] * 3
        args += list(rope_tabs)
    chunks = pl.BlockSpec((tm // CHUNK, w, CHUNK), lambda i: (i, 0, 0))
    out_specs = [chunks, row(w), chunks, row(bw)]
    out_shape = [jax.ShapeDtypeStruct((m // CHUNK, w, CHUNK), BF16), jax.ShapeDtypeStruct((m, w), BF16),
                 jax.ShapeDtypeStruct((m // CHUNK, w, CHUNK), BF16), jax.ShapeDtypeStruct((m, bw), BF16)]
    if emit_kv:
        out_specs += [row(w), row(w)]
        out_shape += [jax.ShapeDtypeStruct((m, w), F32)] * 2
    return pl.pallas_call(
        functools.partial(_in_proj0_body, rope=rope, emit_kv=emit_kv),
        grid=(m // tm,),
        in_specs=in_specs,
        out_specs=out_specs,
        out_shape=out_shape,
        compiler_params=_cparams("parallel"),
        name="in_proj_even",
    )(*args)


def _ctx_prep_body(k_ref, v_ref, ko_ref, vt_ref):
    ko_ref[0] = k_ref[0].astype(BF16)
    _transposed_chunks(v_ref[0], vt_ref)


def _ctx_prep(ctx_k, ctx_v):
    b, p, w = ctx_k.shape
    return pl.pallas_call(
        _ctx_prep_body,
        grid=(b,),
        in_specs=[pl.BlockSpec((1, p, w), lambda bi: (bi, 0, 0))] * 2,
        out_specs=[pl.BlockSpec((1, p, w), lambda bi: (bi, 0, 0)),
                   pl.BlockSpec((p // CHUNK, w, CHUNK), lambda bi: (bi, 0, 0))],
        out_shape=[jax.ShapeDtypeStruct((b, p, w), BF16), jax.ShapeDtypeStruct((b * p // CHUNK, w, CHUNK), BF16)],
        compiler_params=_cparams("parallel"),
        name="ctx_prep",
    )(ctx_k, ctx_v)


def _joined(ref, first, count):
    return jnp.concatenate([ref[first + r] for r in range(count)], axis=1)


def _attn_body(*refs, has_ctx, tk, lam_init):
    if has_ctx:
        qt_ref, k_ref, vt_ref, ck_ref, cvt_ref, lam_ref, sw_ref, o_ref = refs
    else:
        qt_ref, k_ref, vt_ref, lam_ref, sw_ref, o_ref = refs
    qt = _joined(qt_ref, 0, qt_ref.shape[0])
    tq = qt.shape[1]
    sub = lax.broadcasted_iota(jnp.int32, qt.shape, 0)
    zero = jnp.zeros_like(qt)
    qq = jnp.concatenate([jnp.where(sub < A_QK, qt, zero), jnp.where(sub >= A_QK, qt, zero)], axis=1)

    m = jnp.full((1, 2 * tq), NEG_BIG, F32)
    l = jnp.zeros((1, 2 * tq), F32)
    acc = jnp.zeros((A_V, 2 * tq), F32)

    sources = ([(ck_ref, cvt_ref)] if has_ctx else []) + [(k_ref, vt_ref)]
    for k_src, vt_src in sources:
        lk = k_src.shape[1]
        ck = _pick_tile(lk, tk)
        per = ck // CHUNK
        for c in range(lk // ck):
            s = _dot(k_src[0, c * ck:(c + 1) * ck, :], qq)
            m_new = jnp.maximum(m, jnp.max(s, axis=0, keepdims=True))
            alpha = jnp.exp2(m - m_new)
            p = jnp.exp2(s - m_new)
            l = alpha * l + jnp.sum(p, axis=0, keepdims=True)
            acc = alpha * acc + _dot(_joined(vt_src, c * per, per), p.astype(BF16))
            m = m_new

    o = acc / l
    lam = (jnp.exp(jnp.sum(lam_ref[0:1, :] * lam_ref[1:2, :], axis=-1, keepdims=True))
           - jnp.exp(jnp.sum(lam_ref[2:3, :] * lam_ref[3:4, :], axis=-1, keepdims=True)) + lam_init)
    a = o[:, 0:tq] - lam * o[:, tq:]
    an = a * lax.rsqrt(jnp.mean(a * a, axis=0, keepdims=True) + EPS) * sw_ref[...] * (1.0 - lam_init)
    o_ref[0] = an.T.astype(o_ref.dtype)


def _attention(qt, k, vt, ctx, lam4, subln_w, lam_init, tq, tk):
    b, l, w = k.shape
    has_ctx = ctx is not None
    k_spec = lambda n: pl.BlockSpec((1, n, A_V), lambda bi, h, i: (bi, 0, h))
    vt_spec = lambda n: pl.BlockSpec((n // CHUNK, A_V, CHUNK), lambda bi, h, i: (bi, h, 0))
    nq = l // tq
    in_specs = [pl.BlockSpec((tq // CHUNK, A_V, CHUNK), lambda bi, h, i: (bi * nq + i, h, 0)), k_spec(l), vt_spec(l)]
    args = [qt, k, vt]
    if has_ctx:
        p = ctx[0].shape[1]
        in_specs += [k_spec(p), vt_spec(p)]
        args += list(ctx)
    in_specs += [pl.BlockSpec((4, A_QK), lambda bi, h, i: (0, 0)),
                 pl.BlockSpec((A_V, 1), lambda bi, h, i: (0, 0))]
    args += [lam4, subln_w.reshape(A_V, 1)]
    return pl.pallas_call(
        functools.partial(_attn_body, has_ctx=has_ctx, tk=tk, lam_init=lam_init),
        grid=(b, A_HEADS, l // tq),
        in_specs=in_specs,
        out_specs=pl.BlockSpec((1, tq, A_V), lambda bi, h, i: (bi, i, h)),
        out_shape=jax.ShapeDtypeStruct((b, l, w), BF16),
        compiler_params=_cparams("parallel", "parallel", "parallel"),
        name="diff_attention",
    )(*args)


def _mlstm_mix(hf, hb, og, xc, hn_ref, sk_ref, wo_ref):
    hd = hn_ref.shape[1] // C_HEADS
    hs = hf + hb
    hn = jnp.concatenate([_rms(hs[:, h * hd:(h + 1) * hd], hn_ref[:, h * hd:(h + 1) * hd]) for h in range(C_HEADS)],
                         axis=1)
    return _dot((_sigmoid(og) * (hn + sk_ref[...] * xc)).astype(BF16), wo_ref[...])


def _even_mix(a, g, wo_ref):
    aw = a.shape[1]
    return _dot(a.astype(BF16), wo_ref[0:aw, :]) + _dot(g.astype(BF16), wo_ref[aw:, :])


def _ffn_body(*refs, seq_len, fc, final, n_acts):
    it = iter(refs)
    x_ref, xp_ref, xn_ref, nw_ref, mod_ref, wa_ref, wg_ref, cw_ref, cb_ref, wd_ref = [next(it) for _ in range(10)]
    fw_ref = next(it) if final else None
    acts = [[next(it) for _ in range(3)] for _ in range(n_acts)]
    params = [next(it) for _ in range(3 if n_acts == 4 else 1)]
    o_ref = next(it)
    tm, d = x_ref.shape
    dff = wa_ref.shape[1]
    xe = jnp.concatenate([xp_ref[...], x_ref[...], xn_ref[...]], axis=0)
    ext = [jnp.concatenate([prev[...].astype(F32)[-HALO:], main[...].astype(F32), nxt[...].astype(F32)[:HALO]], axis=0)
           for main, prev, nxt in acts]
    xe = xe + mod_ref[0, 2:3, :] * (_mlstm_mix(*ext, *params) if n_acts == 4 else _even_mix(*ext, *params))
    x = xe[HALO:HALO + tm]
    he = _modulated(xe, nw_ref[...], mod_ref[0, 3:4, :], mod_ref[0, 4:5, :]).astype(BF16)
    hc = he[HALO:HALO + tm]
    row0 = pl.program_id(0) * tm
    acc = jnp.zeros((tm, d), F32)
    for c in range(dff // fc):
        cs = slice(c * fc, (c + 1) * fc)
        ae = _dot(he, wa_ref[:, cs])
        g = _dot(hc, wg_ref[:, cs])
        conv = _conv3(ae, tm, seq_len, row0, cw_ref.at[:, cs], cb_ref.at[:, cs])
        acc = acc + _dot((_gelu(conv) * g).astype(BF16), wd_ref[cs, :])
    out = x + mod_ref[0, 5:6, :] * acc
    if final:
        out = _rms(out, fw_ref[...])
    o_ref[...] = out


def _halo_specs(tm, m, width, rows=HALO):
    nb = tm // rows
    last = m // rows - 1
    return [pl.BlockSpec((rows, width), lambda i: (jnp.maximum(i * nb - 1, 0), 0)),
            pl.BlockSpec((rows, width), lambda i: (jnp.minimum((i + 1) * nb, last), 0))]


def _conv_ffn(x, nw, mod, wa, wg, cw, cb, wd, fw, premix, rows_per_mod, seq_len, tm, fc):
    m, d = x.shape
    dff = wa.shape[1]
    final = fw is not None
    const = lambda i: (0, 0)
    in_specs = ([pl.BlockSpec((tm, d), lambda i: (i, 0))] + _halo_specs(tm, m, d)
                + [pl.BlockSpec((1, d), const),
                   pl.BlockSpec((1, N_MOD, d), lambda i: ((i * tm) // rows_per_mod, 0, 0)),
                   pl.BlockSpec((d, dff), const),
                   pl.BlockSpec((d, dff), const),
                   pl.BlockSpec((3, dff), const),
                   pl.BlockSpec((1, dff), const),
                   pl.BlockSpec((dff, d), const)])
    args = [x, x, x, nw.reshape(1, d), mod, wa, wg, cw, cb.reshape(1, dff), wd]
    if final:
        in_specs.append(pl.BlockSpec((1, d), const))
        args.append(fw.reshape(1, d))
    acts, params = premix
    for act in acts:
        in_specs += [pl.BlockSpec((tm, act.shape[1]), lambda i: (i, 0))] + _halo_specs(tm, m, act.shape[1], 2 * HALO)
        args += [act, act, act]
    for prm in params:
        prm = prm.reshape(1, -1) if prm.ndim == 1 else prm
        in_specs.append(pl.BlockSpec(prm.shape, const))
        args.append(prm)
    return pl.pallas_call(
        functools.partial(_ffn_body, seq_len=seq_len, fc=fc, final=final, n_acts=len(acts)),
        grid=(m // tm,),
        in_specs=in_specs,
        out_specs=pl.BlockSpec((tm, d), lambda i: (i, 0)),
        out_shape=jax.ShapeDtypeStruct((m, d), F32),
        compiler_params=_cparams("parallel"),
        name="conv_ffn",
    )(*args)


def _in_proj1_body(x_ref, xp_ref, xn_ref, nw_ref, mod_ref, wm_ref, wr_ref, cw_ref, cb_ref, wq_ref, wk_ref, wv_ref,
                   q_ref, kt_ref, v_ref, xc_ref, og_ref, gt_ref, *, seq_len):
    tm = x_ref.shape[0]
    ci = wm_ref.shape[1]
    hd = ci // C_HEADS
    xe = jnp.concatenate([xp_ref[...], x_ref[...], xn_ref[...]], axis=0)
    he = _modulated(xe, nw_ref[...], mod_ref[0, 0:1, :], mod_ref[0, 1:2, :]).astype(BF16)
    xme = _dot(he, wm_ref[...])
    rest = _dot(he[HALO:HALO + tm], wr_ref[...])
    og_ref[...] = rest[:, 0:ci].astype(BF16)
    gt_ref[...] = rest[:, ci:]
    conv = _conv3(xme, tm, seq_len, pl.program_id(0) * tm, cw_ref, cb_ref)
    xc = conv * _sigmoid(conv)
    xcb = xc.astype(BF16)
    xc_ref[...] = xcb
    xmb = xme[HALO:HALO + tm].astype(BF16)
    for h in range(C_HEADS):
        hs = slice(h * hd, (h + 1) * hd)
        q_ref[:, hs] = _dot(xcb[:, hs], wq_ref[h]).astype(BF16)
        k = _dot(xcb[:, hs], wk_ref[h]) * hd ** -0.5
        for r in range(tm // SCAN_T):
            kt_ref[r, hs, :] = k[r * SCAN_T:(r + 1) * SCAN_T].T.astype(BF16)
        v_ref[:, hs] = _dot(xmb[:, hs], wv_ref[h]).astype(BF16)


def _in_proj1(x, nw, mod, wm, wr, cw, cb, wq, wk, wv, rows_per_mod, seq_len, tm):
    m, d = x.shape
    ci = wm.shape[1]
    ng = wr.shape[1] - ci
    const = lambda i: (0, 0)
    const3 = lambda i: (0, 0, 0)
    row = pl.BlockSpec((tm, ci), lambda i: (i, 0))
    return pl.pallas_call(
        functools.partial(_in_proj1_body, seq_len=seq_len),
        grid=(m // tm,),
        in_specs=[pl.BlockSpec((tm, d), lambda i: (i, 0))] + _halo_specs(tm, m, d)
        + [pl.BlockSpec((1, d), const),
           pl.BlockSpec((1, N_MOD, d), lambda i: ((i * tm) // rows_per_mod, 0, 0)),
           pl.BlockSpec(wm.shape, const), pl.BlockSpec(wr.shape, const),
           pl.BlockSpec((3, ci), const), pl.BlockSpec((1, ci), const),
           pl.BlockSpec(wq.shape, const3), pl.BlockSpec(wk.shape, const3), pl.BlockSpec(wv.shape, const3)],
        out_specs=[row, pl.BlockSpec((tm // SCAN_T, ci, SCAN_T), lambda i: (i, 0, 0)), row, row, row,
                   pl.BlockSpec((tm, ng), lambda i: (i, 0))],
        out_shape=[jax.ShapeDtypeStruct((m, ci), BF16), jax.ShapeDtypeStruct((m // SCAN_T, ci, SCAN_T), BF16)]
        + [jax.ShapeDtypeStruct((m, ci), BF16)] * 3 + [jax.ShapeDtypeStruct((m, ng), F32)],
        compiler_params=_cparams("parallel"),
        name="in_proj_odd",
    )(x, x, x, nw.reshape(1, d), mod, wm, wr, cw, cb.reshape(1, ci), wq, wk, wv)


def _split3(x):
    hi = x.astype(BF16)
    r1 = x - hi.astype(F32)
    mid = r1.astype(BF16)
    lo = (r1 - mid.astype(F32)).astype(BF16)
    return hi, mid, lo


def _mlstm_scan_body(*refs, zero_init, cpb):
    it = iter(refs)
    fwd = [next(it) for _ in range(4)]
    bwd = [next(it) for _ in range(4)]
    bg_ref = next(it)
    init = None if zero_init else [next(it) for _ in range(3)]
    hf_ref, hb_ref, c_ref, n_ref, m_ref, cx_ref = [next(it) for _ in range(6)]
    t = SCAN_T
    hd = c_ref.shape[-1]
    nw = cx_ref.shape[-1] - hd
    wide = lambda a: jnp.concatenate([a] * (hd // nw), axis=1)
    j = pl.program_id(1)

    @pl.when(j == 0)
    def _():
        for dr in range(2):
            for h in range(C_HEADS):
                if zero_init:
                    cx_ref[dr, h] = jnp.zeros(cx_ref.shape[2:], F32)
                else:
                    cx_ref[dr, h, :, 0:hd] = init[0][0, dr, h]
                    cx_ref[dr, h, :, hd:] = jnp.broadcast_to(init[1][0, dr, h], (nw, hd)).T
        m_ref[...] = jnp.zeros_like(m_ref) if zero_init else init[2][...]

    rr = lax.broadcasted_iota(jnp.int32, (t, t), 0)
    cc = lax.broadcasted_iota(jnp.int32, (t, t), 1)
    ones = jnp.ones((t, nw), BF16)

    srcs = ((fwd, hf_ref), (bwd, hb_ref))
    chunk_of = lambda ci, dr: ci if dr == 0 else cpb - 1 - ci

    m_cur = {(dr, h): m_ref[0, dr, h][:, 0:1] for dr in range(2) for h in range(C_HEADS)}
    pre = {}
    for ci in range(cpb):
        for dr in range(2):
            g_ref = srcs[dr][0][3]
            ch = chunk_of(ci, dr)
            mask = (rr >= cc) if dr == 0 else (rr <= cc)
            tri = jnp.where(mask, 1.0, 0.0).astype(BF16)
            gates = g_ref[0, ch * t:(ch + 1) * t, :] + bg_ref[...]
            hi, mid, lo = _split3(_log_sigmoid(gates))
            bsum = _dot(tri, hi) + _dot(tri, mid) + _dot(tri, lo)
            bsum_t = bsum.T
            gates_t = gates.T
            for h in range(C_HEADS):
                col_i = 2 * dr * C_HEADS + h
                col_f = (2 * dr + 1) * C_HEADS + h
                b_col = jnp.broadcast_to(bsum[:, col_f:col_f + 1], (t, t))
                b_row = bsum_t[col_f:col_f + 1, :]
                i_row = gates_t[col_i:col_i + 1, :]
                m_prev = m_cur[dr, h]
                a_inter = b_col + m_prev
                dmat = jnp.where(mask, b_col - b_row + i_row, -jnp.inf)
                m_t = jnp.maximum(a_inter, jnp.max(dmat, axis=1, keepdims=True))
                b_end = b_row[:, t - 1:t] if dr == 0 else b_row[:, 0:1]
                g_row = b_end - b_row + i_row
                m_new = jnp.maximum(b_end + m_prev, jnp.max(g_row, axis=1, keepdims=True))
                m_cur[dr, h] = m_new
                pre[ci, dr, h] = (jnp.where(mask, jnp.exp(dmat - m_t), 0.0),
                                  jnp.exp(a_inter - m_t)[:, 0:nw],
                                  jnp.exp(-m_t)[:, 0:nw],
                                  jnp.exp(b_end + m_prev - m_new),
                                  jnp.exp(g_row - m_new))

    for ci in range(cpb):
        for dr in range(2):
            (q_ref, kt_ref, v_ref, _), h_out = srcs[dr]
            ch = chunk_of(ci, dr)
            rows = slice(ch * t, (ch + 1) * t)
            for h in range(C_HEADS):
                decay_w, w_inter, floor, decay, w_row = pre[ci, dr, h]
                hs = slice(h * hd, (h + 1) * hd)
                cx = cx_ref[dr, h]
                qc = q_ref[0, rows, hs]
                v1 = jnp.concatenate([v_ref[0, rows, hs], ones], axis=1)
                ktc = kt_ref[ch, hs, :]
                s = _dot(qc, ktc) * decay_w
                intra = _dot(s.astype(BF16), v1[:, 0:hd])
                inter = _dot(qc, cx.astype(BF16))
                den = jnp.sum(s, axis=1, keepdims=True) + w_inter * inter[:, hd:]
                inv = 1.0 / jnp.maximum(jnp.abs(den), floor)
                num = intra + wide(w_inter) * inter[:, 0:hd]
                h_out[0, rows, hs] = (num * wide(inv)).astype(h_out.dtype)
                kw_t = (ktc.astype(F32) * w_row).astype(BF16)
                cx_ref[dr, h] = decay * cx + _dot(kw_t, v1)

    for (dr, h), m_new in m_cur.items():
        m_ref[0, dr, h] = jnp.broadcast_to(m_new, (1, m_ref.shape[-1]))

    @pl.when(j == pl.num_programs(1) - 1)
    def _():
        for dr in range(2):
            for h in range(C_HEADS):
                c_ref[0, dr, h] = cx_ref[dr, h, :, 0:hd]
                n_ref[0, dr, h] = cx_ref[dr, h, :, hd:].T[0:1, :]


def _mlstm_scan(q, kt, v, gates, bg, init, lb):
    b, l, ci = q.shape
    hd = ci // C_HEADS
    nb = l // lb
    cpb = lb // SCAN_T
    zero_init = init is None
    in_specs, args = [], []
    for blk in (lambda j: j, lambda j: nb - 1 - j):
        row = pl.BlockSpec((1, lb, ci), lambda bi, j, blk=blk: (bi, blk(j), 0))
        in_specs += [row,
                     pl.BlockSpec((cpb, ci, SCAN_T), lambda bi, j, blk=blk: (bi * nb + blk(j), 0, 0)),
                     row,
                     pl.BlockSpec((1, lb, gates.shape[2]), lambda bi, j, blk=blk: (bi, blk(j), 0))]
        args += [q, kt, v, gates]
    in_specs.append(pl.BlockSpec((1, 128), lambda bi, j: (0, 0)))
    args.append(bg)
    st = lambda *tail: pl.BlockSpec((1, 2, C_HEADS) + tail, lambda bi, j: (bi, 0, 0, 0, 0))
    states = [st(hd, hd), st(1, hd), st(1, 128)]
    if not zero_init:
        in_specs += states
        args += list(init)
    return pl.pallas_call(
        functools.partial(_mlstm_scan_body, zero_init=zero_init, cpb=cpb),
        grid=(b, nb),
        in_specs=in_specs,
        out_specs=[pl.BlockSpec((1, lb, ci), lambda bi, j: (bi, j, 0)),
                   pl.BlockSpec((1, lb, ci), lambda bi, j: (bi, nb - 1 - j, 0))] + states,
        out_shape=[jax.ShapeDtypeStruct((b, l, ci), BF16),
                   jax.ShapeDtypeStruct((b, l, ci), BF16),
                   jax.ShapeDtypeStruct((b, 2, C_HEADS, hd, hd), F32),
                   jax.ShapeDtypeStruct((b, 2, C_HEADS, 1, hd), F32),
                   jax.ShapeDtypeStruct((b, 2, C_HEADS, 1, 128), F32)],
        scratch_shapes=[pltpu.VMEM((2, C_HEADS, hd, hd + CHUNK), F32)],
        compiler_params=_cparams("parallel", "arbitrary"),
        name="mlstm_scan",
    )(*args)


def _rope_tables(length):
    n_rows = length // GRID_W
    rows = jnp.repeat(jnp.arange(n_rows, dtype=F32), GRID_W)
    cols = jnp.tile(jnp.arange(GRID_W, dtype=F32), n_rows)
    inv = ROPE_BASE ** (-jnp.arange(N_FREQ, dtype=F32) / N_FREQ)
    ang = jnp.stack([rows[:, None] * inv, cols[:, None] * inv], axis=1)
    cos, sin = jnp.cos(ang), jnp.sin(ang)
    zeros = jnp.zeros_like(sin)
    lanes = lambda first, second: jnp.tile(
        jnp.stack([first, second], axis=2).reshape(length, A_QK), (1, A_V // A_QK))
    return lanes(cos, cos), lanes(-sin, zeros), lanes(zeros, sin)


def _pick_tile(n, want):
    t = min(n, want)
    while n % t:
        t //= 2
    return t


def _run_group(x3, mods, ctx, init, p, tiles):
    b, l, d = x3.shape
    m = b * l
    rows_per_mod = l if mods.shape[1] > 1 else m
    tm = _pick_tile(l if mods.shape[1] > 1 else m, tiles["tm"])
    x = x3.reshape(m, d)
    extras = {}
    depth = mods.shape[0]
    for layer in range(depth):
        mod = mods[layer]
        if layer % 2 == 0:
            e = layer // 2
            lam_init = 0.8 - 0.6 * math.exp(-0.3 * layer)
            rope = _rope_tables(l) if ctx is not None else None
            outs = _in_proj0(x, p["norm1_w"][layer], mod, p["w_in0"][e], p["gate_norm_w"][e], p["w_spatial"][e],
                             p["b_spatial"][e].T, rope, ctx is None, rows_per_mod, l, tm)
            qt, k, vt, g = outs[:4]
            ctx_e = _ctx_prep(ctx[0][:, e].reshape(b, -1, A_WIDTH), ctx[1][:, e].reshape(b, -1, A_WIDTH)) \
                if ctx is not None else None
            lam4 = jnp.stack([p["lam_q1"][e], p["lam_k1"][e], p["lam_q2"][e], p["lam_k2"][e]])
            a = _attention(qt, k.reshape(b, l, A_WIDTH), vt, ctx_e, lam4, p["subln_w"][e],
                           lam_init, _pick_tile(l, tiles["tq"]), tiles["tk"])
            premix = ((a.reshape(m, A_WIDTH), g), (p["w_out0"][e],))
            for name, kv in zip("kv", outs[4:]):
                extras.setdefault(name, []).append(kv.reshape(b, l, A_HEADS, A_V))
        else:
            o = layer // 2
            ci = p["w_out1"].shape[1]
            q, kt, v, xc, og, gates = _in_proj1(x, p["norm1_w"][layer], mod, p["w_in1_m"][o], p["w_in1_r"][o],
                                                p["mconv_w"][o], p["mconv_b"][o], p["w_q"][o], p["w_k"][o],
                                                p["w_v"][o], rows_per_mod, l, tm)
            init_o = None if init is None else tuple(s[:, o] for s in init)
            hf, hb, c_fin, n_fin, m_fin = _mlstm_scan(q.reshape(b, l, ci), kt, v.reshape(b, l, ci),
                                                      gates.reshape(b, l, -1), p["b_gates"][o], init_o,
                                                      _pick_tile(l, tiles["lb"]))
            premix = ((hf.reshape(m, ci), hb.reshape(m, ci), og, xc),
                      (p["head_norm_w"][o], p["skip_w"][o], p["w_out1"][o]))
            extras.setdefault("C", []).append(c_fin)
            extras.setdefault("n", []).append(n_fin[:, :, :, 0, :])
            extras.setdefault("m", []).append(m_fin[:, :, :, 0, 0])
        fw = p["final_norm_w"] if layer == depth - 1 else None
        x = _conv_ffn(x, p["norm2_w"][layer], mod, p["w_up_a"][layer], p["w_up_g"][layer], p["fconv_w"][layer],
                      p["fconv_b"][layer], p["w_down"][layer], fw, premix, rows_per_mod, l, tm, tiles["fc"])
    return x.reshape(b, l, d), extras


def kernel(x_prompt, x_sample, cache_k, cache_v, state_C, state_n, state_m, c, c_ctx, w_mod, b_mod, norm1_w, norm2_w,
           w_in0, lam_q1, lam_k1, lam_q2, lam_k2, subln_w, gate_norm_w, w_spatial, b_spatial, w_out0, w_in1,
           b_gates, mconv_w, mconv_b, w_q, w_k, w_v, head_norm_w, skip_w, w_out1, w_up, fconv_w, fconv_b, w_down,
           final_norm_w):
    depth, d, _ = w_mod.shape
    dec_b = x_sample.shape[0]
    dff = w_down.shape[1]
    ci = w_out1.shape[1]

    n_cond = -(-(dec_b + 1) // 8) * 8
    cond = jnp.zeros((n_cond, d), F32).at[:dec_b].set(c).at[dec_b].set(c_ctx)
    mods = _modulation(cond, w_mod, b_mod).reshape(depth, n_cond, N_MOD, d)

    n_in1 = w_in1.shape[2]
    pad1 = -(-n_in1 // 128) * 128 - n_in1
    p = dict(
        norm1_w=norm1_w, norm2_w=norm2_w, lam_q1=lam_q1, lam_k1=lam_k1, lam_q2=lam_q2, lam_k2=lam_k2,
        subln_w=subln_w, gate_norm_w=gate_norm_w, b_spatial=b_spatial, mconv_w=mconv_w, mconv_b=mconv_b,
        head_norm_w=head_norm_w, skip_w=skip_w, fconv_w=fconv_w, fconv_b=fconv_b, final_norm_w=final_norm_w,
        w_in0=w_in0.astype(BF16), w_spatial=w_spatial.astype(BF16), w_out0=w_out0.astype(BF16),
        w_in1_m=w_in1[:, :, :ci].astype(BF16),
        w_in1_r=jnp.pad(w_in1[:, :, ci:], ((0, 0), (0, 0), (0, pad1))).astype(BF16),
        b_gates=jnp.pad(b_gates, ((0, 0), (0, 128 - b_gates.shape[1])))[:, None, :],
        w_q=w_q.astype(BF16), w_k=w_k.astype(BF16), w_v=w_v.astype(BF16), w_out1=w_out1.astype(BF16),
        w_up_a=w_up[:, :, :dff].astype(BF16), w_up_g=w_up[:, :, dff:].astype(BF16), w_down=w_down.astype(BF16),
    )
    tiles = dict(tm=512, tq=1024, tk=512, fc=2816, lb=512)

    y_prompt, ex = _run_group(x_prompt, mods[:, dec_b:dec_b + 1], None, None, p, tiles)
    init = (state_C.astype(F32),
            state_n.astype(F32)[:, :, :, :, None, :],
            jnp.broadcast_to(state_m.astype(F32)[:, :, :, :, None, None], state_m.shape + (1, 128)))
    y_sample, _ = _run_group(x_sample, mods[:, :dec_b], (cache_k, cache_v), init, p, tiles)

    return (y_prompt, y_sample, jnp.stack(ex["k"], axis=1), jnp.stack(ex["v"], axis=1),
            jnp.stack(ex["C"], axis=1), jnp.stack(ex["n"], axis=1), jnp.stack(ex["m"], axis=1))
```
